```python
import jax, jax.numpy as jnp
from jax import lax
import numpy as np

D_MODEL = 2048
BATCH = 4
SEQ = 4096
DEPTH = 2

GDN_HEADS = 8
GDN_HEAD_DIM = 128
GDN_WIDTH = GDN_HEADS * GDN_HEAD_DIM
GDN_CHUNK = 64
CONV_WIDTH = 4
LRU_WIDTH = 1024
LRU_BLOCKS = 8
LRU_BLOCK_DIM = LRU_WIDTH // LRU_BLOCKS
LRU_C = 8.0
D_FF = 4 * D_MODEL
N_BRANCHES = 2
RMS_EPS = 1e-6
L2_EPS = 1e-6

SPLIT_SIZES = (3 * GDN_WIDTH, GDN_WIDTH, GDN_HEADS, GDN_HEADS, LRU_WIDTH, LRU_WIDTH, D_MODEL, D_MODEL)
IN_COLS = 3 * GDN_WIDTH + GDN_WIDTH + 2 * GDN_HEADS + 2 * LRU_WIDTH + N_BRANCHES * D_MODEL

kernel_name = 'hybrid_gdn_rglru_gated_merge'


def rms_norm(x, gain):
    xf = x.astype(jnp.float32)
    y = xf * lax.rsqrt(jnp.mean(xf * xf, axis=-1, keepdims=True) + RMS_EPS)
    return (y * gain.astype(jnp.float32)).astype(x.dtype)


def l2_normalize(x):
    return x * lax.rsqrt(jnp.sum(x * x, axis=-1, keepdims=True) + L2_EPS)


def causal_depthwise_conv(x, w):
    width = w.shape[0]
    seq = x.shape[1]
    xp = jnp.pad(x, ((0, 0), (width - 1, 0), (0, 0)))
    y = xp[:, 0:seq, :] * w[0]
    for j in range(1, width):
        y = y + xp[:, j:j + seq, :] * w[j]
    return y


def gated_delta_rule_chunked(q, k, v, g, beta):
    bsz, heads, seq, dk = q.shape
    dv = v.shape[-1]
    c = GDN_CHUNK
    n = seq // c
    q = q * (dk ** -0.5)

    def chunks(t):
        return t.reshape((bsz, heads, n, c) + t.shape[3:])

    q, k, v, g, beta = chunks(q), chunks(k), chunks(v), chunks(g), chunks(beta)
    g = jnp.cumsum(g, axis=-1)
    k_beta = k * beta[..., None]
    v_beta = v * beta[..., None]
    causal = jnp.tril(jnp.ones((c, c), dtype=bool))
    strict = jnp.tril(jnp.ones((c, c), dtype=bool), -1)
    diff = g[..., :, None] - g[..., None, :]
    decay = jnp.where(causal, jnp.exp(jnp.where(causal, diff, 0.0)), 0.0)
    lower = jnp.where(strict, jnp.einsum('bhnid,bhnjd->bhnij', k_beta, k) * decay, 0.0)
    eye = jnp.eye(c, dtype=q.dtype)
    t_inv = lax.linalg.triangular_solve(eye + lower, jnp.broadcast_to(eye, lower.shape),
                                        left_side=True, lower=True, unit_diagonal=True)
    u = jnp.einsum('bhnij,bhnjv->bhniv', t_inv, v_beta)
    w = jnp.einsum('bhnij,bhnjk->bhnik', t_inv, k_beta * jnp.exp(g)[..., None])
    qk = jnp.where(causal, jnp.einsum('bhnid,bhnjd->bhnij', q, k) * decay, 0.0)
    q_dec = q * jnp.exp(g)[..., None]
    g_last = g[..., -1]
    k_dec = k * jnp.exp(g_last[..., None] - g)[..., None]

    xs = (jnp.moveaxis(u, 2, 0), jnp.moveaxis(w, 2, 0), jnp.moveaxis(qk, 2, 0),
          jnp.moveaxis(q_dec, 2, 0), jnp.moveaxis(k_dec, 2, 0), jnp.moveaxis(g_last, 2, 0))

    def step(state, inp):
        u_n, w_n, qk_n, qd_n, kd_n, gl_n = inp
        v_new = u_n - jnp.einsum('bhck,bhkv->bhcv', w_n, state)
        out = (jnp.einsum('bhck,bhkv->bhcv', qd_n, state)
               + jnp.einsum('bhij,bhjv->bhiv', qk_n, v_new))
        state = state * jnp.exp(gl_n)[..., None, None] + jnp.einsum('bhck,bhcv->bhkv', kd_n, v_new)
        return state, out

    state0 = jnp.zeros((bsz, heads, dk, dv), dtype=q.dtype)
    _, out = lax.scan(step, state0, xs)
    return jnp.moveaxis(out, 0, 2).reshape(bsz, heads, seq, dv)


def gated_deltanet_branch(qkv, z, a, b, conv_w, a_log, dt_bias, norm_gain):
    bsz, seq, _ = qkv.shape
    f32 = jnp.float32
    qkv = jax.nn.silu(causal_depthwise_conv(qkv, conv_w))
    q, k, v = jnp.split(qkv, 3, axis=-1)

    def to_heads(t):
        return t.reshape(bsz, seq, GDN_HEADS, GDN_HEAD_DIM).transpose(0, 2, 1, 3).astype(f32)

    q = l2_normalize(to_heads(q))
    k = l2_normalize(to_heads(k))
    v = to_heads(v)
    g = -jnp.exp(a_log.astype(f32)) * jax.nn.softplus(a.astype(f32) + dt_bias.astype(f32))
    beta = jax.nn.sigmoid(b.astype(f32))
    o = gated_delta_rule_chunked(q, k, v, g.transpose(0, 2, 1), beta.transpose(0, 2, 1))
    o = rms_norm(o.transpose(0, 2, 1, 3), norm_gain)
    o = o.astype(z.dtype) * jax.nn.silu(z).reshape(bsz, seq, GDN_HEADS, GDN_HEAD_DIM)
    return o.reshape(bsz, seq, GDN_WIDTH)


def rglru_branch(xb, yb, conv_w, conv_b, w_gate_a, b_gate_a, w_gate_x, b_gate_x, lam):
    bsz, seq, _ = xb.shape
    f32 = jnp.float32
    xc = causal_depthwise_conv(xb, conv_w) + conv_b
    xh = xc.reshape(bsz, seq, LRU_BLOCKS, LRU_BLOCK_DIM)
    r = jax.nn.sigmoid(jnp.einsum('bsni,nij->bsnj', xh, w_gate_a).reshape(bsz, seq, LRU_WIDTH) + b_gate_a)
    i = jax.nn.sigmoid(jnp.einsum('bsni,nij->bsnj', xh, w_gate_x).reshape(bsz, seq, LRU_WIDTH) + b_gate_x)
    log_a = -LRU_C * r.astype(f32) * jax.nn.softplus(-lam.astype(f32))
    a = jnp.exp(log_a)
    bterm = jnp.sqrt(-jnp.expm1(2.0 * log_a)) * (i * xc).astype(f32)

    def combine(left, right):
        a_l, b_l = left
        a_r, b_r = right
        return a_l * a_r, a_r * b_l + b_r

    _, h = lax.associative_scan(combine, (a, bterm), axis=1)
    return h.astype(xb.dtype) * jax.nn.gelu(yb)


def hybrid_layer(x, attn_norm, w_in, gdn_conv_w, gdn_a_log, gdn_dt_bias, gdn_norm,
                 lru_conv_w, lru_conv_b, lru_w_a, lru_b_a, lru_w_x, lru_b_x, lru_lambda,
                 w_branch_gdn, w_branch_lru, w_out, mlp_norm, w_up, w_down):
    h = rms_norm(x, attn_norm)
    proj = h @ w_in
    offsets = []
    acc = 0
    for s in SPLIT_SIZES[:-1]:
        acc += s
        offsets.append(acc)
    qkv, z, a, b, xb, yb, gate_gdn, gate_lru = jnp.split(proj, offsets, axis=-1)
    o_gdn = gated_deltanet_branch(qkv, z, a, b, gdn_conv_w, gdn_a_log, gdn_dt_bias, gdn_norm)
    o_lru = rglru_branch(xb, yb, lru_conv_w, lru_conv_b, lru_w_a, lru_b_a, lru_w_x, lru_b_x, lru_lambda)
    merged = (jax.nn.sigmoid(gate_gdn) * (o_gdn @ w_branch_gdn)
              + jax.nn.sigmoid(gate_lru) * (o_lru @ w_branch_lru))
    x = x + merged @ w_out
    h = rms_norm(x, mlp_norm)
    x = x + jnp.square(jax.nn.relu(h @ w_up)) @ w_down
    return x


def setup_inputs(seed: int = 0) -> dict:
    key = jax.random.key(seed)
    ks = jax.random.split(key, 24)
    f32 = jnp.float32
    d, l = D_MODEL, DEPTH

    def normal(k, shape, scale):
        return jax.random.normal(k, shape, f32) * scale

    def gain(k, shape):
        return 1.0 + 0.02 * jax.random.normal(k, shape, f32)

    x = normal(ks[0], (BATCH, SEQ, d), 1.0)
    attn_norm = gain(ks[1], (l, d))
    w_in = normal(ks[2], (l, d, IN_COLS), d ** -0.5)
    gdn_conv_w = normal(ks[3], (l, CONV_WIDTH, 3 * GDN_WIDTH), CONV_WIDTH ** -0.5)
    gdn_a_log = jnp.log(jax.random.uniform(ks[4], (l, GDN_HEADS), f32, 1.0, 16.0))
    dt = jnp.exp(jax.random.uniform(ks[5], (l, GDN_HEADS), f32, np.log(1e-3), np.log(1e-1)))
    gdn_dt_bias = dt + jnp.log(-jnp.expm1(-dt))
    gdn_norm = gain(ks[6], (l, GDN_HEAD_DIM))
    lru_conv_w = normal(ks[7], (l, CONV_WIDTH, LRU_WIDTH), CONV_WIDTH ** -0.5)
    lru_conv_b = normal(ks[8], (l, LRU_WIDTH), 0.01)
    lru_w_a = normal(ks[9], (l, LRU_BLOCKS, LRU_BLOCK_DIM, LRU_BLOCK_DIM), LRU_BLOCK_DIM ** -0.5)
    lru_b_a = normal(ks[10], (l, LRU_WIDTH), 0.01)
    lru_w_x = normal(ks[11], (l, LRU_BLOCKS, LRU_BLOCK_DIM, LRU_BLOCK_DIM), LRU_BLOCK_DIM ** -0.5)
    lru_b_x = normal(ks[12], (l, LRU_WIDTH), 0.01)
    a0 = jax.random.uniform(ks[13], (l, LRU_WIDTH), f32, 0.9, 0.999)
    s0 = a0 ** (1.0 / LRU_C)
    lru_lambda = jnp.log(s0) - jnp.log1p(-s0)
    w_branch_gdn = normal(ks[14], (l, GDN_WIDTH, d), GDN_WIDTH ** -0.5)
    w_branch_lru = normal(ks[15], (l, LRU_WIDTH, d), LRU_WIDTH ** -0.5)
    w_out = normal(ks[16], (l, d, d), d ** -0.5)
    mlp_norm = gain(ks[17], (l, d))
    w_up = normal(ks[18], (l, d, D_FF), d ** -0.5)
    w_down = normal(ks[19], (l, D_FF, d), D_FF ** -0.5)
    final_norm = gain(ks[20], (d,))
    return {'x': x, 'attn_norm': attn_norm, 'w_in': w_in, 'gdn_conv_w': gdn_conv_w,
            'gdn_a_log': gdn_a_log, 'gdn_dt_bias': gdn_dt_bias, 'gdn_norm': gdn_norm,
            'lru_conv_w': lru_conv_w, 'lru_conv_b': lru_conv_b, 'lru_w_a': lru_w_a, 'lru_b_a': lru_b_a,
            'lru_w_x': lru_w_x, 'lru_b_x': lru_b_x, 'lru_lambda': lru_lambda,
            'w_branch_gdn': w_branch_gdn, 'w_branch_lru': w_branch_lru, 'w_out': w_out,
            'mlp_norm': mlp_norm, 'w_up': w_up, 'w_down': w_down, 'final_norm': final_norm}


def reference(x, attn_norm, w_in, gdn_conv_w, gdn_a_log, gdn_dt_bias, gdn_norm,
              lru_conv_w, lru_conv_b, lru_w_a, lru_b_a, lru_w_x, lru_b_x, lru_lambda,
              w_branch_gdn, w_branch_lru, w_out, mlp_norm, w_up, w_down, final_norm):
    for layer in range(DEPTH):
        x = hybrid_layer(x, attn_norm[layer], w_in[layer], gdn_conv_w[layer], gdn_a_log[layer],
                         gdn_dt_bias[layer], gdn_norm[layer], lru_conv_w[layer], lru_conv_b[layer],
                         lru_w_a[layer], lru_b_a[layer], lru_w_x[layer], lru_b_x[layer], lru_lambda[layer],
                         w_branch_gdn[layer], w_branch_lru[layer], w_out[layer], mlp_norm[layer],
                         w_up[layer], w_down[layer])
    return rms_norm(x, final_norm)
```

```python
import functools

import jax
import jax.numpy as jnp
from jax import lax
from jax.experimental import pallas as pl
from jax.experimental.pallas import tpu as pltpu

F32 = jnp.float32
BF16 = jnp.bfloat16
HIGHEST = lax.Precision.HIGHEST

RMS_EPS = 1e-6
L2_EPS = 1e-6
LRU_C = 8.0
CONV_WIDTH = 4
GDN_CHUNK = 64
INV_BLOCK = 16
LANES = 128
SUBLANES = 8
VMEM_LIMIT = 56 * 1024 * 1024


def _mm(a, b):
    return jnp.dot(a.astype(BF16), b.astype(BF16), preferred_element_type=F32)


def _mm_nt(a, b):
    return lax.dot_general(a.astype(BF16), b.astype(BF16), (((1,), (1,)), ((), ())),
                           preferred_element_type=F32)


def _mm_tn(a, b):
    return lax.dot_general(a.astype(BF16), b.astype(BF16), (((0,), (0,)), ((), ())),
                           preferred_element_type=F32)


def _mm_hi(a, b):
    return jnp.dot(a, b, preferred_element_type=F32, precision=HIGHEST)


def _softplus(x):
    return jnp.maximum(x, 0.0) + jnp.log1p(jnp.exp(-jnp.abs(x)))


def _silu(x):
    return x * jax.nn.sigmoid(x)


def _gelu_tanh(x):
    c = 0.7978845608028654
    return 0.5 * x * (1.0 + jnp.tanh(c * (x + 0.044715 * (x * x * x))))


def _in_proj_kernel(x_ref, g_ref, w_ref, wab_ref, o_ref, oab_ref, h_ref):
    @pl.when(pl.program_id(1) == 0)
    def _():
        x = x_ref[...]
        ms = jnp.mean(x * x, axis=-1, keepdims=True)
        h = (x * lax.rsqrt(ms + RMS_EPS) * g_ref[...]).astype(BF16)
        h_ref[...] = h
        oab_ref[...] = jnp.dot(h, wab_ref[...], preferred_element_type=F32)

    o_ref[...] = jnp.dot(h_ref[...], w_ref[...], preferred_element_type=F32).astype(o_ref.dtype)


def _in_proj(x2, gain, w_main, w_ab, tm, tn):
    m, d = x2.shape
    n = w_main.shape[1]
    return pl.pallas_call(
        _in_proj_kernel,
        grid=(m // tm, n // tn),
        in_specs=[
            pl.BlockSpec((tm, d), lambda i, j: (i, 0)),
            pl.BlockSpec((1, d), lambda i, j: (0, 0)),
            pl.BlockSpec((d, tn), lambda i, j: (0, j)),
            pl.BlockSpec((d, LANES), lambda i, j: (0, 0)),
        ],
        out_specs=[
            pl.BlockSpec((tm, tn), lambda i, j: (i, j)),
            pl.BlockSpec((tm, LANES), lambda i, j: (i, 0)),
        ],
        out_shape=[
            jax.ShapeDtypeStruct((m, n), BF16),
            jax.ShapeDtypeStruct((m, LANES), F32),
        ],
        scratch_shapes=[pltpu.VMEM((tm, d), BF16)],
        compiler_params=pltpu.CompilerParams(
            dimension_semantics=("arbitrary", "arbitrary"), vmem_limit_bytes=VMEM_LIMIT),
        name="in_proj",
    )(x2, gain, w_main, w_ab)


def _tri_inverse(low):
    c = low.shape[0]
    row = lax.broadcasted_iota(jnp.int32, (c, c), 0)
    col = lax.broadcasted_iota(jnp.int32, (c, c), 1)
    eye = (row == col).astype(F32)
    same_block = (row & -INV_BLOCK) == (col & -INV_BLOCK)
    ld = jnp.where(same_block, low, 0.0)
    lo = low - ld
    dinv = eye - ld
    pw = ld
    span = 2
    while span < INV_BLOCK:
        pw = _mm_hi(pw, pw)
        dinv = dinv + _mm_hi(dinv, pw)
        span *= 2
    n = _mm_hi(dinv, lo)
    acc = eye - n
    pw = n
    span = 2
    while span < c // INV_BLOCK:
        pw = _mm_hi(pw, pw)
        acc = acc + _mm_hi(acc, pw)
        span *= 2
    return _mm_hi(acc, dinv)


def _gdn_kernel(qkv_ref, z_ref, ab_ref, cw_ref, alog_ref, dtb_ref, gain_ref, o_ref,
                xbuf, qn, kn, vn, state, *, heads, ts):
    hd = LANES
    width = heads * hd
    c = GDN_CHUNK
    t = pl.program_id(1)

    @pl.when(t == 0)
    def _():
        xbuf[0:SUBLANES, :] = jnp.zeros((SUBLANES, 3 * width), F32)
        state[...] = jnp.zeros_like(state)

    xbuf[SUBLANES:SUBLANES + ts, :] = qkv_ref[...].astype(F32)
    base = SUBLANES - (CONV_WIDTH - 1)
    scale = hd ** -0.5
    for part, dst in enumerate((qn, kn, vn)):
        for h in range(heads):
            lo = part * width + h * hd
            acc = xbuf[base:base + ts, lo:lo + hd] * cw_ref[0:1, lo:lo + hd]
            for j in range(1, CONV_WIDTH):
                acc = acc + xbuf[base + j:base + j + ts, lo:lo + hd] * cw_ref[j:j + 1, lo:lo + hd]
            y = _silu(acc)
            if part < 2:
                y = y * lax.rsqrt(jnp.sum(y * y, axis=-1, keepdims=True) + L2_EPS)
            if part == 0:
                y = y * scale
            dst[h] = y
    xbuf[0:SUBLANES, :] = xbuf[ts:ts + SUBLANES, :]

    ab = ab_ref[...]
    g_all = -jnp.exp(alog_ref[...]) * _softplus(ab + dtb_ref[...])
    beta_all = jax.nn.sigmoid(ab)

    row = lax.broadcasted_iota(jnp.int32, (c, c), 0)
    col = lax.broadcasted_iota(jnp.int32, (c, c), 1)
    causal = row >= col
    strict = row > col
    tril = causal.astype(F32)
    gain = gain_ref[...]

    for ci in range(ts // c):
        r0 = ci * c
        gcum = _mm_hi(tril, g_all[r0:r0 + c, :])
        gcum_t = gcum.T
        beta_c = beta_all[r0:r0 + c, :]
        for h in range(heads):
            q = qn[h, r0:r0 + c, :]
            k = kn[h, r0:r0 + c, :]
            v = vn[h, r0:r0 + c, :]
            gcol = gcum[:, h:h + 1]
            grow = gcum_t[h:h + 1, :]
            glast = gcum[c - 1:c, h:h + 1]
            beta = beta_c[:, heads + h:heads + h + 1]
            decay = jnp.where(causal, jnp.exp(jnp.where(causal, gcol - grow, 0.0)), 0.0)
            egc = jnp.exp(gcol)
            kb = k * beta
            prod = _mm_nt(jnp.concatenate([q, kb], axis=0), k)
            qk = jnp.where(causal, prod[0:c] * decay, 0.0)
            low = jnp.where(strict, prod[c:2 * c] * decay, 0.0)
            t_inv = _tri_inverse(low)
            uw = _mm_hi(t_inv, jnp.concatenate([v * beta, kb * egc], axis=1))
            u = uw[:, 0:hd]
            w = uw[:, hd:2 * hd]
            s = state[h]
            ws = _mm(jnp.concatenate([w, q * egc], axis=0), s)
            v_new = u - ws[0:c]
            out = ws[c:2 * c] + _mm(qk, v_new)
            kd = k * jnp.exp(glast - gcol)
            state[h] = s * jnp.exp(glast) + _mm_tn(kd, v_new)
            o = out * lax.rsqrt(jnp.mean(out * out, axis=-1, keepdims=True) + RMS_EPS) * gain
            zz = z_ref[r0:r0 + c, h * hd:(h + 1) * hd].astype(F32)
            o_ref[r0:r0 + c, h * hd:(h + 1) * hd] = (o * _silu(zz)).astype(o_ref.dtype)


def _gdn(proj, ab, conv_w, alog_row, dtb_row, gain_row, batch, seq, heads, ts):
    width = heads * LANES
    nt = seq // ts
    kern = functools.partial(_gdn_kernel, heads=heads, ts=ts)
    return pl.pallas_call(
        kern,
        grid=(batch, nt),
        in_specs=[
            pl.BlockSpec((ts, 3 * width), lambda b, t: (b * nt + t, 0)),
            pl.BlockSpec((ts, width), lambda b, t: (b * nt + t, 3)),
            pl.BlockSpec((ts, LANES), lambda b, t: (b * nt + t, 0)),
            pl.BlockSpec((CONV_WIDTH, 3 * width), lambda b, t: (0, 0)),
            pl.BlockSpec((1, LANES), lambda b, t: (0, 0)),
            pl.BlockSpec((1, LANES), lambda b, t: (0, 0)),
            pl.BlockSpec((1, LANES), lambda b, t: (0, 0)),
        ],
        out_specs=pl.BlockSpec((ts, width), lambda b, t: (b * nt + t, 0)),
        out_shape=jax.ShapeDtypeStruct((batch * seq, width), BF16),
        scratch_shapes=[
            pltpu.VMEM((ts + 2 * SUBLANES, 3 * width), F32),
            pltpu.VMEM((heads, ts, LANES), F32),
            pltpu.VMEM((heads, ts, LANES), F32),
            pltpu.VMEM((heads, ts, LANES), F32),
            pltpu.VMEM((heads, LANES, LANES), F32),
        ],
        compiler_params=pltpu.CompilerParams(
            dimension_semantics=("arbitrary", "arbitrary"), vmem_limit_bytes=VMEM_LIMIT),
        name="gdn",
    )(proj, proj, ab, conv_w, alog_row, dtb_row, gain_row)


def _lru_kernel(xb_ref, yb_ref, cw_ref, cb_ref, wa_ref, ba_ref, wx_ref, bx_ref, lam_ref, o_ref,
                xbuf, abuf, bbuf, hcar, *, blocks, ts):
    bd = LANES
    width = blocks * bd
    t = pl.program_id(1)

    @pl.when(t == 0)
    def _():
        xbuf[0:SUBLANES, :] = jnp.zeros((SUBLANES, width), F32)
        hcar[...] = jnp.zeros_like(hcar)

    xbuf[SUBLANES:SUBLANES + ts, :] = xb_ref[...].astype(F32)
    base = SUBLANES - (CONV_WIDTH - 1)
    for n in range(blocks):
        lo = n * bd
        xc = xbuf[base:base + ts, lo:lo + bd] * cw_ref[0:1, lo:lo + bd]
        for j in range(1, CONV_WIDTH):
            xc = xc + xbuf[base + j:base + j + ts, lo:lo + bd] * cw_ref[j:j + 1, lo:lo + bd]
        xc = xc + cb_ref[0:1, lo:lo + bd]
        xcb = xc.astype(BF16)
        r = jax.nn.sigmoid(jnp.dot(xcb, wa_ref[n], preferred_element_type=F32) + ba_ref[0:1, lo:lo + bd])
        i = jax.nn.sigmoid(jnp.dot(xcb, wx_ref[n], preferred_element_type=F32) + bx_ref[0:1, lo:lo + bd])
        log_a = (-LRU_C) * r * _softplus(-lam_ref[0:1, lo:lo + bd])
        a = jnp.exp(log_a)
        one_minus_a2 = -jnp.tanh(log_a) * (1.0 + a * a)
        abuf[:, lo:lo + bd] = a
        bbuf[:, lo:lo + bd] = jnp.sqrt(one_minus_a2) * (i * xc)
    xbuf[0:SUBLANES, :] = xbuf[ts:ts + SUBLANES, :]

    rowi = lax.broadcasted_iota(jnp.int32, (SUBLANES, width), 0)

    def group(gi, hprev):
        r0 = pl.multiple_of(gi * SUBLANES, SUBLANES)
        a = abuf[pl.ds(r0, SUBLANES), :]
        b = bbuf[pl.ds(r0, SUBLANES), :]
        s = 1
        while s < SUBLANES:
            keep = rowi >= s
            a_sh = jnp.where(keep, pltpu.roll(a, s, axis=0), 1.0)
            b_sh = jnp.where(keep, pltpu.roll(b, s, axis=0), 0.0)
            b = a * b_sh + b
            a = a * a_sh
            s *= 2
        bbuf[pl.ds(r0, SUBLANES), :] = a * hprev + b
        return a[SUBLANES - 1:SUBLANES, :] * hprev + b[SUBLANES - 1:SUBLANES, :]

    hcar[...] = lax.fori_loop(0, ts // SUBLANES, group, hcar[...], unroll=4)
    o_ref[...] = (bbuf[...] * _gelu_tanh(yb_ref[...].astype(F32))).astype(o_ref.dtype)


def _lru(proj, conv_w, conv_b, w_a, b_a, w_x, b_x, lam, batch, seq, blocks, xb_col, yb_col, ts):
    width = blocks * LANES
    nt = seq // ts
    kern = functools.partial(_lru_kernel, blocks=blocks, ts=ts)
    row = lambda b, t: (0, 0)
    return pl.pallas_call(
        kern,
        grid=(batch, nt),
        in_specs=[
            pl.BlockSpec((ts, width), lambda b, t: (b * nt + t, xb_col)),
            pl.BlockSpec((ts, width), lambda b, t: (b * nt + t, yb_col)),
            pl.BlockSpec((CONV_WIDTH, width), row),
            pl.BlockSpec((1, width), row),
            pl.BlockSpec((blocks, LANES, LANES), lambda b, t: (0, 0, 0)),
            pl.BlockSpec((1, width), row),
            pl.BlockSpec((blocks, LANES, LANES), lambda b, t: (0, 0, 0)),
            pl.BlockSpec((1, width), row),
            pl.BlockSpec((1, width), row),
        ],
        out_specs=pl.BlockSpec((ts, width), lambda b, t: (b * nt + t, 0)),
        out_shape=jax.ShapeDtypeStruct((batch * seq, width), BF16),
        scratch_shapes=[
            pltpu.VMEM((ts + 2 * SUBLANES, width), F32),
            pltpu.VMEM((ts, width), F32),
            pltpu.VMEM((ts, width), F32),
            pltpu.VMEM((1, width), F32),
        ],
        compiler_params=pltpu.CompilerParams(
            dimension_semantics=("arbitrary", "arbitrary"), vmem_limit_bytes=VMEM_LIMIT),
        name="lru",
    )(proj, proj, conv_w, conv_b, w_a, b_a, w_x, b_x, lam)


def _merge_kernel(x_ref, og_ref, ol_ref, gg_ref, gl_ref, wg_ref, wl_ref, wo_ref, o_ref):
    pg = jnp.dot(og_ref[...], wg_ref[...], preferred_element_type=F32)
    pb = jnp.dot(ol_ref[...], wl_ref[...], preferred_element_type=F32)
    merged = (jax.nn.sigmoid(gg_ref[...].astype(F32)) * pg
              + jax.nn.sigmoid(gl_ref[...].astype(F32)) * pb)
    o_ref[...] = x_ref[...] + jnp.dot(merged.astype(BF16), wo_ref[...], preferred_element_type=F32)


def _merge(x2, o_gdn, o_lru, proj, w_bg, w_bl, w_out, gg_col, gl_col, tm):
    m, d = x2.shape
    wg = o_gdn.shape[1]
    wl = o_lru.shape[1]
    const = lambda i: (0, 0)
    return pl.pallas_call(
        _merge_kernel,
        grid=(m // tm,),
        in_specs=[
            pl.BlockSpec((tm, d), lambda i: (i, 0)),
            pl.BlockSpec((tm, wg), lambda i: (i, 0)),
            pl.BlockSpec((tm, wl), lambda i: (i, 0)),
            pl.BlockSpec((tm, d), lambda i: (i, gg_col)),
            pl.BlockSpec((tm, d), lambda i: (i, gl_col)),
            pl.BlockSpec((wg, d), const, pipeline_mode=pl.Buffered(1)),
            pl.BlockSpec((wl, d), const, pipeline_mode=pl.Buffered(1)),
            pl.BlockSpec((d, d), const, pipeline_mode=pl.Buffered(1)),
        ],
        out_specs=pl.BlockSpec((tm, d), lambda i: (i, 0)),
        out_shape=jax.ShapeDtypeStruct((m, d), F32),
        compiler_params=pltpu.CompilerParams(
            dimension_semantics=("arbitrary",), vmem_limit_bytes=VMEM_LIMIT),
        name="merge",
    )(x2, o_gdn, o_lru, proj, proj, w_bg, w_bl, w_out)


def _mlp_kernel(x_ref, g_ref, wu_ref, wd_ref, fg_ref, o_ref, h_ref, *, final_norm):
    f = pl.program_id(1)

    @pl.when(f == 0)
    def _():
        x = x_ref[...]
        ms = jnp.mean(x * x, axis=-1, keepdims=True)
        h_ref[...] = (x * lax.rsqrt(ms + RMS_EPS) * g_ref[...]).astype(BF16)
        o_ref[...] = x

    u = jnp.dot(h_ref[...], wu_ref[...], preferred_element_type=F32)
    u = jnp.maximum(u, 0.0)
    o_ref[...] += jnp.dot((u * u).astype(BF16), wd_ref[...], preferred_element_type=F32)

    if final_norm:
        @pl.when(f == pl.num_programs(1) - 1)
        def _():
            y = o_ref[...]
            ms = jnp.mean(y * y, axis=-1, keepdims=True)
            o_ref[...] = y * lax.rsqrt(ms + RMS_EPS) * fg_ref[...]


def _mlp(x2, gain, w_up, w_down, final_gain, final_norm, tm, tf):
    m, d = x2.shape
    ff = w_up.shape[1]
    kern = functools.partial(_mlp_kernel, final_norm=final_norm)
    return pl.pallas_call(
        kern,
        grid=(m // tm, ff // tf),
        in_specs=[
            pl.BlockSpec((tm, d), lambda i, f: (i, 0)),
            pl.BlockSpec((1, d), lambda i, f: (0, 0)),
            pl.BlockSpec((d, tf), lambda i, f: (0, f)),
            pl.BlockSpec((tf, d), lambda i, f: (f, 0)),
            pl.BlockSpec((1, d), lambda i, f: (0, 0)),
        ],
        out_specs=pl.BlockSpec((tm, d), lambda i, f: (i, 0)),
        out_shape=jax.ShapeDtypeStruct((m, d), F32),
        scratch_shapes=[pltpu.VMEM((tm, d), BF16)],
        compiler_params=pltpu.CompilerParams(
            dimension_semantics=("arbitrary", "arbitrary"), vmem_limit_bytes=VMEM_LIMIT),
        name="mlp",
    )(x2, gain, w_up, w_down, final_gain)


def _pad_row(v, n):
    return jnp.pad(v.astype(F32), (0, n - v.shape[0])).reshape(1, n)


def _tile(n, want):
    t = min(want, n)
    assert n % t == 0, (n, want)
    return t


def kernel(x, attn_norm, w_in, gdn_conv_w, gdn_a_log, gdn_dt_bias, gdn_norm, lru_conv_w, lru_conv_b,
           lru_w_a, lru_b_a, lru_w_x, lru_b_x, lru_lambda, w_branch_gdn, w_branch_lru, w_out,
           mlp_norm, w_up, w_down, final_norm):
    batch, seq, d = x.shape
    depth = w_in.shape[0]
    heads = gdn_a_log.shape[1]
    gw = heads * LANES
    assert gdn_norm.shape[1] == LANES and gdn_conv_w.shape[2] == 3 * gw
    blocks = lru_w_a.shape[1]
    lw = blocks * LANES
    assert lru_w_a.shape[2] == LANES
    assert 2 * heads <= LANES and seq % GDN_CHUNK == 0
    assert gw == lw
    m = batch * seq

    o_z = 3 * gw
    o_a = o_z + gw
    o_xb = o_a + 2 * heads
    o_yb = o_xb + lw
    o_gg = o_yb + lw
    o_gl = o_gg + d
    assert w_in.shape[2] == o_gl + d
    xb_col = (o_a) // lw
    yb_col = xb_col + 1
    gg_col = (o_a + 2 * lw) // d
    gl_col = gg_col + 1
    assert o_a % lw == 0 and (o_a + 2 * lw) % d == 0

    tm_in = _tile(m, 1024)
    tn_in = _tile(o_gl + d - 2 * heads, 1024)
    ts_gdn = _tile(seq, 256)
    ts_lru = _tile(seq, 512)
    tm_merge = _tile(m, 256)
    tm_mlp = _tile(m, 1024)
    tf_mlp = _tile(w_up.shape[2], 512)

    x2 = x.reshape(m, d)
    for l in range(depth):
        wl = w_in[l]
        w_main = jnp.concatenate([wl[:, :o_a], wl[:, o_xb:]], axis=1).astype(BF16)
        w_ab = jnp.pad(wl[:, o_a:o_xb], ((0, 0), (0, LANES - 2 * heads))).astype(BF16)
        proj, ab = _in_proj(x2, attn_norm[l].reshape(1, d), w_main, w_ab, tm_in, tn_in)

        o_gdn = _gdn(proj, ab, gdn_conv_w[l], _pad_row(gdn_a_log[l], LANES), _pad_row(gdn_dt_bias[l], LANES),
                     gdn_norm[l].reshape(1, LANES), batch, seq, heads, ts_gdn)
        o_lru = _lru(proj, lru_conv_w[l], lru_conv_b[l].reshape(1, lw), lru_w_a[l].astype(BF16),
                     lru_b_a[l].reshape(1, lw), lru_w_x[l].astype(BF16), lru_b_x[l].reshape(1, lw),
                     lru_lambda[l].reshape(1, lw), batch, seq, blocks, xb_col, yb_col, ts_lru)
        x2 = _merge(x2, o_gdn, o_lru, proj, w_branch_gdn[l].astype(BF16), w_branch_lru[l].astype(BF16),
                    w_out[l].astype(BF16), gg_col, gl_col, tm_merge)
        last = l == depth - 1
        x2 = _mlp(x2, mlp_norm[l].reshape(1, d), w_up[l].astype(BF16), w_down[l].astype(BF16),
                  final_norm.reshape(1, d), last, tm_mlp, tf_mlp)
    if depth == 0:
        raise ValueError("depth must be positive")
    return x2.reshape(batch, seq, d)
```

```python
import functools

import jax
import jax.numpy as jnp
from jax import lax
from jax.experimental import pallas as pl
from jax.experimental.pallas import tpu as pltpu

F32 = jnp.float32
BF16 = jnp.bfloat16
HIGHEST = lax.Precision.HIGHEST

RMS_EPS = 1e-6
L2_EPS = 1e-6
LRU_C = 8.0
CONV_WIDTH = 4
GDN_CHUNK = 64
INV_BLOCK = 16
GDN_PACK = 4
LANES = 128
SUBLANES = 8
VMEM_LIMIT = 56 * 1024 * 1024


def _mm(a, b):
    return jnp.dot(a.astype(BF16), b.astype(BF16), preferred_element_type=F32)


def _mm_nt(a, b):
    return lax.dot_general(a.astype(BF16), b.astype(BF16), (((1,), (1,)), ((), ())),
                           preferred_element_type=F32)


def _mm_tn(a, b):
    return lax.dot_general(a.astype(BF16), b.astype(BF16), (((0,), (0,)), ((), ())),
                           preferred_element_type=F32)


def _mm_hi(a, b):
    return jnp.dot(a, b, preferred_element_type=F32, precision=HIGHEST)


def _softplus(x):
    return jnp.maximum(x, 0.0) + jnp.log1p(jnp.exp(-jnp.abs(x)))


def _silu(x):
    return x * jax.nn.sigmoid(x)


def _gelu_tanh(x):
    c = 0.7978845608028654
    return 0.5 * x * (1.0 + jnp.tanh(c * (x + 0.044715 * (x * x * x))))


def _in_proj_kernel(x_ref, g_ref, w_ref, wab_ref, o_ref, oab_ref, h_ref):
    @pl.when(pl.program_id(1) == 0)
    def _():
        x = x_ref[...]
        ms = jnp.mean(x * x, axis=-1, keepdims=True)
        h = (x * lax.rsqrt(ms + RMS_EPS) * g_ref[...]).astype(BF16)
        h_ref[...] = h
        oab_ref[...] = jnp.dot(h, wab_ref[...], preferred_element_type=F32)

    o_ref[...] = jnp.dot(h_ref[...], w_ref[...], preferred_element_type=F32).astype(o_ref.dtype)


def _in_proj(x2, gain, w_main, w_ab, tm, tn):
    m, d = x2.shape
    n = w_main.shape[1]
    return pl.pallas_call(
        _in_proj_kernel,
        grid=(m // tm, n // tn),
        in_specs=[
            pl.BlockSpec((tm, d), lambda i, j: (i, 0)),
            pl.BlockSpec((1, d), lambda i, j: (0, 0)),
            pl.BlockSpec((d, tn), lambda i, j: (0, j)),
            pl.BlockSpec((d, LANES), lambda i, j: (0, 0)),
        ],
        out_specs=[
            pl.BlockSpec((tm, tn), lambda i, j: (i, j)),
            pl.BlockSpec((tm, LANES), lambda i, j: (i, 0)),
        ],
        out_shape=[
            jax.ShapeDtypeStruct((m, n), BF16),
            jax.ShapeDtypeStruct((m, LANES), F32),
        ],
        scratch_shapes=[pltpu.VMEM((tm, d), BF16)],
        compiler_params=pltpu.CompilerParams(
            dimension_semantics=("arbitrary", "arbitrary"), vmem_limit_bytes=VMEM_LIMIT),
        name="in_proj",
    )(x2, gain, w_main, w_ab)


def _gdn_kernel(qkv_ref, z_ref, ab_ref, cw_ref, alog_ref, dtb_ref, gain_ref, o_ref,
                xbuf, qn, kn, vn, state, qk_s, u_s, w_s, qd_s, kd_s, *, heads):
    hd = LANES
    width = heads * hd
    c = GDN_CHUNK
    ts = GDN_PACK * c
    t = pl.program_id(1)
    hs = range(heads)

    @pl.when(t == 0)
    def _():
        xbuf[0:SUBLANES, :] = jnp.zeros((SUBLANES, 3 * width), F32)
        state[...] = jnp.zeros_like(state)

    xbuf[SUBLANES:SUBLANES + ts, :] = qkv_ref[...].astype(F32)
    base = SUBLANES - (CONV_WIDTH - 1)
    scale = hd ** -0.5
    for part, dst in enumerate((qn, kn, vn)):
        for h in hs:
            lo = part * width + h * hd
            acc = xbuf[base:base + ts, lo:lo + hd] * cw_ref[0:1, lo:lo + hd]
            for j in range(1, CONV_WIDTH):
                acc = acc + xbuf[base + j:base + j + ts, lo:lo + hd] * cw_ref[j:j + 1, lo:lo + hd]
            y = _silu(acc)
            if part < 2:
                y = y * lax.rsqrt(jnp.sum(y * y, axis=-1, keepdims=True) + L2_EPS)
            if part == 0:
                y = y * scale
            dst[h] = y
    xbuf[0:SUBLANES, :] = xbuf[ts:ts + SUBLANES, :]

    ab = ab_ref[...]
    g_all = -jnp.exp(alog_ref[...]) * _softplus(ab + dtb_ref[...])
    beta_all = jax.nn.sigmoid(ab)

    r2 = lax.broadcasted_iota(jnp.int32, (ts, ts), 0)
    c2 = lax.broadcasted_iota(jnp.int32, (ts, ts), 1)
    same_chunk = (r2 & -c) == (c2 & -c)
    causal_bd = same_chunk & (r2 >= c2)
    strict_bd = same_chunk & (r2 > c2)
    bd_mask = jnp.where(same_chunk, 1.0, 0.0).astype(BF16)
    rp = lax.broadcasted_iota(jnp.int32, (c, ts), 0)
    cp = lax.broadcasted_iota(jnp.int32, (c, ts), 1) & (c - 1)
    eye_p = jnp.where(rp == cp, 1.0, 0.0)
    in_diag_block = (rp & -INV_BLOCK) == (cp & -INV_BLOCK)

    def blockdiag(y):
        yb = y.astype(BF16)
        return jnp.concatenate([yb] * GDN_PACK, axis=0) * bd_mask

    def bmm(x, bd):
        return jnp.dot(x.astype(BF16), bd, preferred_element_type=F32)

    sums = _mm_hi(jnp.concatenate([jnp.where(causal_bd, 1.0, 0.0), jnp.where(same_chunk, 1.0, 0.0)], axis=0), g_all)
    gcum = sums[0:ts]
    glast = sums[ts:2 * ts]
    gcum_t = gcum.T

    prods = []
    for h in hs:
        k = kn[h]
        kb = k * beta_all[:, heads + h:heads + h + 1]
        prods.append(_mm_nt(jnp.concatenate([qn[h], kb], axis=0), k))
    lps = []
    for h in hs:
        gcol = gcum[:, h:h + 1]
        grow = gcum_t[h:h + 1, :]
        dmat = jnp.where(causal_bd, jnp.exp(jnp.where(causal_bd, gcol - grow, 0.0)), 0.0)
        qk_s[h] = prods[h][0:ts] * dmat
        lbd = jnp.where(strict_bd, prods[h][ts:2 * ts] * dmat, 0.0)
        lp = lbd[0:c]
        for i in range(1, GDN_PACK):
            lp = lp + lbd[i * c:(i + 1) * c]
        lps.append(lp)

    ld = [jnp.where(in_diag_block, lp, 0.0) for lp in lps]
    lo = [lp - d for lp, d in zip(lps, ld)]
    dinv = [eye_p - d for d in ld]
    pw = [bmm(d, blockdiag(d)) for d in ld]
    span = 2
    while span < INV_BLOCK:
        last = 2 * span >= INV_BLOCK
        res = [bmm(d if last else jnp.concatenate([p, d], axis=0), blockdiag(p)) for p, d in zip(pw, dinv)]
        if last:
            dinv = [d + r for d, r in zip(dinv, res)]
        else:
            pw = [r[0:c] for r in res]
            dinv = [d + r[c:2 * c] for d, r in zip(dinv, res)]
        span *= 2
    n = [bmm(d, blockdiag(l)) for d, l in zip(dinv, lo)]
    acc = [eye_p - x for x in n]
    pw = n
    span = 2
    while span < c // INV_BLOCK:
        pw = [bmm(p, blockdiag(p)) for p in pw]
        acc = [a + bmm(a, blockdiag(p)) for a, p in zip(acc, pw)]
        span *= 2
    t0 = [bmm(a, blockdiag(d)) for a, d in zip(acc, dinv)]
    t_hi = [x.astype(BF16) for x in t0]
    t_lo = [x - xh.astype(F32) for x, xh in zip(t0, t_hi)]
    a_mat = [eye_p + lp for lp in lps]
    a_hi = [x.astype(BF16) for x in a_mat]
    a_lo = [(x - xh.astype(F32)).astype(BF16) for x, xh in zip(a_mat, a_hi)]
    bd_hi = [blockdiag(x) for x in t_hi]
    r1 = [jnp.dot(jnp.concatenate([ah, al], axis=0), bd, preferred_element_type=F32)
          for ah, al, bd in zip(a_hi, a_lo, bd_hi)]
    r2_ = [jnp.dot(ah, blockdiag(tl), preferred_element_type=F32) for ah, tl in zip(a_hi, t_lo)]
    resid = [eye_p - (x[0:c] + x[c:2 * c] + y) for x, y in zip(r1, r2_)]
    tinv = [x + jnp.dot(xh, blockdiag(r), preferred_element_type=F32) for x, xh, r in zip(t0, t_hi, resid)]

    for h in hs:
        q = qn[h]
        k = kn[h]
        gcol = gcum[:, h:h + 1]
        beta = beta_all[:, heads + h:heads + h + 1]
        egc = jnp.exp(gcol)
        rhs = jnp.concatenate([vn[h] * beta, k * (beta * egc)], axis=1)
        uw = jnp.dot(blockdiag(tinv[h]), rhs.astype(BF16), preferred_element_type=F32)
        u_s[h] = uw[:, 0:hd]
        w_s[h] = uw[:, hd:2 * hd]
        qd_s[h] = q * egc
        kd_s[h] = k * jnp.exp(glast[:, h:h + 1] - gcol)

    gain = gain_ref[...]
    zero_v = jnp.zeros((c, hd), F32)
    for ci in range(GDN_PACK):
        r0 = ci * c
        slab = (r0 // hd) * hd
        wss = [_mm(jnp.concatenate([w_s[h, r0:r0 + c, :], qd_s[h, r0:r0 + c, :]], axis=0), state[h]) for h in hs]
        for h in hs:
            v_new = u_s[h, r0:r0 + c, :] - wss[h][0:c]
            pieces = [zero_v] * (hd // c)
            pieces[(r0 - slab) // c] = v_new
            out = wss[h][c:2 * c] + _mm(qk_s[h, r0:r0 + c, slab:slab + hd], jnp.concatenate(pieces, axis=0))
            state[h] = (state[h] * jnp.exp(glast[r0:r0 + 1, h:h + 1])
                        + _mm_tn(kd_s[h, r0:r0 + c, :], v_new))
            o = out * lax.rsqrt(jnp.mean(out * out, axis=-1, keepdims=True) + RMS_EPS) * gain
            zz = z_ref[r0:r0 + c, h * hd:(h + 1) * hd].astype(F32)
            o_ref[r0:r0 + c, h * hd:(h + 1) * hd] = (o * _silu(zz)).astype(o_ref.dtype)


def _gdn(proj, ab, conv_w, alog_row, dtb_row, gain_row, batch, seq, heads):
    width = heads * LANES
    ts = GDN_PACK * GDN_CHUNK
    nt = seq // ts
    kern = functools.partial(_gdn_kernel, heads=heads)
    return pl.pallas_call(
        kern,
        grid=(batch, nt),
        in_specs=[
            pl.BlockSpec((ts, 3 * width), lambda b, t: (b * nt + t, 0)),
            pl.BlockSpec((ts, width), lambda b, t: (b * nt + t, 3)),
            pl.BlockSpec((ts, LANES), lambda b, t: (b * nt + t, 0)),
            pl.BlockSpec((CONV_WIDTH, 3 * width), lambda b, t: (0, 0)),
            pl.BlockSpec((1, LANES), lambda b, t: (0, 0)),
            pl.BlockSpec((1, LANES), lambda b, t: (0, 0)),
            pl.BlockSpec((1, LANES), lambda b, t: (0, 0)),
        ],
        out_specs=pl.BlockSpec((ts, width), lambda b, t: (b * nt + t, 0)),
        out_shape=jax.ShapeDtypeStruct((batch * seq, width), BF16),
        scratch_shapes=[
            pltpu.VMEM((ts + 2 * SUBLANES, 3 * width), F32),
            pltpu.VMEM((heads, ts, LANES), F32),
            pltpu.VMEM((heads, ts, LANES), F32),
            pltpu.VMEM((heads, ts, LANES), F32),
            pltpu.VMEM((heads, LANES, LANES), F32),
            pltpu.VMEM((heads, ts, ts), F32),
            pltpu.VMEM((heads, ts, LANES), F32),
            pltpu.VMEM((heads, ts, LANES), F32),
            pltpu.VMEM((heads, ts, LANES), F32),
            pltpu.VMEM((heads, ts, LANES), F32),
        ],
        compiler_params=pltpu.CompilerParams(
            dimension_semantics=("arbitrary", "arbitrary"), vmem_limit_bytes=VMEM_LIMIT),
        name="gdn",
    )(proj, proj, ab, conv_w, alog_row, dtb_row, gain_row)


def _lru_kernel(xb_ref, yb_ref, cw_ref, cb_ref, wa_ref, ba_ref, wx_ref, bx_ref, lam_ref, o_ref,
                xbuf, abuf, bbuf, hcar, *, blocks, ts):
    bd = LANES
    width = blocks * bd
    t = pl.program_id(1)

    @pl.when(t == 0)
    def _():
        xbuf[0:SUBLANES, :] = jnp.zeros((SUBLANES, width), F32)
        hcar[...] = jnp.zeros_like(hcar)

    xbuf[SUBLANES:SUBLANES + ts, :] = xb_ref[...].astype(F32)
    base = SUBLANES - (CONV_WIDTH - 1)
    for n in range(blocks):
        lo = n * bd
        xc = xbuf[base:base + ts, lo:lo + bd] * cw_ref[0:1, lo:lo + bd]
        for j in range(1, CONV_WIDTH):
            xc = xc + xbuf[base + j:base + j + ts, lo:lo + bd] * cw_ref[j:j + 1, lo:lo + bd]
        xc = xc + cb_ref[0:1, lo:lo + bd]
        xcb = xc.astype(BF16)
        r = jax.nn.sigmoid(jnp.dot(xcb, wa_ref[n], preferred_element_type=F32) + ba_ref[0:1, lo:lo + bd])
        i = jax.nn.sigmoid(jnp.dot(xcb, wx_ref[n], preferred_element_type=F32) + bx_ref[0:1, lo:lo + bd])
        log_a = (-LRU_C) * r * _softplus(-lam_ref[0:1, lo:lo + bd])
        a = jnp.exp(log_a)
        one_minus_a2 = -jnp.tanh(log_a) * (1.0 + a * a)
        abuf[:, lo:lo + bd] = a
        bbuf[:, lo:lo + bd] = jnp.sqrt(one_minus_a2) * (i * xc)
    xbuf[0:SUBLANES, :] = xbuf[ts:ts + SUBLANES, :]

    rowi = lax.broadcasted_iota(jnp.int32, (SUBLANES, width), 0)

    def group(gi, hprev):
        r0 = pl.multiple_of(gi * SUBLANES, SUBLANES)
        a = abuf[pl.ds(r0, SUBLANES), :]
        b = bbuf[pl.ds(r0, SUBLANES), :]
        s = 1
        while s < SUBLANES:
            keep = rowi >= s
            a_sh = jnp.where(keep, pltpu.roll(a, s, axis=0), 1.0)
            b_sh = jnp.where(keep, pltpu.roll(b, s, axis=0), 0.0)
            b = a * b_sh + b
            a = a * a_sh
            s *= 2
        bbuf[pl.ds(r0, SUBLANES), :] = a * hprev + b
        return a[SUBLANES - 1:SUBLANES, :] * hprev + b[SUBLANES - 1:SUBLANES, :]

    hcar[...] = lax.fori_loop(0, ts // SUBLANES, group, hcar[...], unroll=4)
    o_ref[...] = (bbuf[...] * _gelu_tanh(yb_ref[...].astype(F32))).astype(o_ref.dtype)


def _lru(proj, conv_w, conv_b, w_a, b_a, w_x, b_x, lam, batch, seq, blocks, xb_col, yb_col, ts):
    width = blocks * LANES
    nt = seq // ts
    kern = functools.partial(_lru_kernel, blocks=blocks, ts=ts)
    row = lambda b, t: (0, 0)
    return pl.pallas_call(
        kern,
        grid=(batch, nt),
        in_specs=[
            pl.BlockSpec((ts, width), lambda b, t: (b * nt + t, xb_col)),
            pl.BlockSpec((ts, width), lambda b, t: (b * nt + t, yb_col)),
            pl.BlockSpec((CONV_WIDTH, width), row),
            pl.BlockSpec((1, width), row),
            pl.BlockSpec((blocks, LANES, LANES), lambda b, t: (0, 0, 0)),
            pl.BlockSpec((1, width), row),
            pl.BlockSpec((blocks, LANES, LANES), lambda b, t: (0, 0, 0)),
            pl.BlockSpec((1, width), row),
            pl.BlockSpec((1, width), row),
        ],
        out_specs=pl.BlockSpec((ts, width), lambda b, t: (b * nt + t, 0)),
        out_shape=jax.ShapeDtypeStruct((batch * seq, width), BF16),
        scratch_shapes=[
            pltpu.VMEM((ts + 2 * SUBLANES, width), F32),
            pltpu.VMEM((ts, width), F32),
            pltpu.VMEM((ts, width), F32),
            pltpu.VMEM((1, width), F32),
        ],
        compiler_params=pltpu.CompilerParams(
            dimension_semantics=("arbitrary", "arbitrary"), vmem_limit_bytes=VMEM_LIMIT),
        name="lru",
    )(proj, proj, conv_w, conv_b, w_a, b_a, w_x, b_x, lam)


def _merge_kernel(x_ref, og_ref, ol_ref, gg_ref, gl_ref, wg_ref, wl_ref, wo_ref, o_ref):
    pg = jnp.dot(og_ref[...], wg_ref[...], preferred_element_type=F32)
    pb = jnp.dot(ol_ref[...], wl_ref[...], preferred_element_type=F32)
    merged = (jax.nn.sigmoid(gg_ref[...].astype(F32)) * pg
              + jax.nn.sigmoid(gl_ref[...].astype(F32)) * pb)
    o_ref[...] = x_ref[...] + jnp.dot(merged.astype(BF16), wo_ref[...], preferred_element_type=F32)


def _merge(x2, o_gdn, o_lru, proj, w_bg, w_bl, w_out, gg_col, gl_col, tm):
    m, d = x2.shape
    wg = o_gdn.shape[1]
    wl = o_lru.shape[1]
    const = lambda i: (0, 0)
    return pl.pallas_call(
        _merge_kernel,
        grid=(m // tm,),
        in_specs=[
            pl.BlockSpec((tm, d), lambda i: (i, 0)),
            pl.BlockSpec((tm, wg), lambda i: (i, 0)),
            pl.BlockSpec((tm, wl), lambda i: (i, 0)),
            pl.BlockSpec((tm, d), lambda i: (i, gg_col)),
            pl.BlockSpec((tm, d), lambda i: (i, gl_col)),
            pl.BlockSpec((wg, d), const, pipeline_mode=pl.Buffered(1)),
            pl.BlockSpec((wl, d), const, pipeline_mode=pl.Buffered(1)),
            pl.BlockSpec((d, d), const, pipeline_mode=pl.Buffered(1)),
        ],
        out_specs=pl.BlockSpec((tm, d), lambda i: (i, 0)),
        out_shape=jax.ShapeDtypeStruct((m, d), F32),
        compiler_params=pltpu.CompilerParams(
            dimension_semantics=("arbitrary",), vmem_limit_bytes=VMEM_LIMIT),
        name="merge",
    )(x2, o_gdn, o_lru, proj, proj, w_bg, w_bl, w_out)


def _mlp_kernel(x_ref, g_ref, wu_ref, wd_ref, fg_ref, o_ref, h_ref, *, final_norm):
    f = pl.program_id(1)

    @pl.when(f == 0)
    def _():
        x = x_ref[...]
        ms = jnp.mean(x * x, axis=-1, keepdims=True)
        h_ref[...] = (x * lax.rsqrt(ms + RMS_EPS) * g_ref[...]).astype(BF16)
        o_ref[...] = x

    u = jnp.dot(h_ref[...], wu_ref[...], preferred_element_type=F32)
    u = jnp.maximum(u, 0.0)
    o_ref[...] += jnp.dot((u * u).astype(BF16), wd_ref[...], preferred_element_type=F32)

    if final_norm:
        @pl.when(f == pl.num_programs(1) - 1)
        def _():
            y = o_ref[...]
            ms = jnp.mean(y * y, axis=-1, keepdims=True)
            o_ref[...] = y * lax.rsqrt(ms + RMS_EPS) * fg_ref[...]


def _mlp(x2, gain, w_up, w_down, final_gain, final_norm, tm, tf):
    m, d = x2.shape
    ff = w_up.shape[1]
    kern = functools.partial(_mlp_kernel, final_norm=final_norm)
    return pl.pallas_call(
        kern,
        grid=(m // tm, ff // tf),
        in_specs=[
            pl.BlockSpec((tm, d), lambda i, f: (i, 0)),
            pl.BlockSpec((1, d), lambda i, f: (0, 0)),
            pl.BlockSpec((d, tf), lambda i, f: (0, f)),
            pl.BlockSpec((tf, d), lambda i, f: (f, 0)),
            pl.BlockSpec((1, d), lambda i, f: (0, 0)),
        ],
        out_specs=pl.BlockSpec((tm, d), lambda i, f: (i, 0)),
        out_shape=jax.ShapeDtypeStruct((m, d), F32),
        scratch_shapes=[pltpu.VMEM((tm, d), BF16)],
        compiler_params=pltpu.CompilerParams(
            dimension_semantics=("arbitrary", "arbitrary"), vmem_limit_bytes=VMEM_LIMIT),
        name="mlp",
    )(x2, gain, w_up, w_down, final_gain)


def _pad_row(v, n):
    return jnp.pad(v.astype(F32), (0, n - v.shape[0])).reshape(1, n)


def _tile(n, want):
    t = min(want, n)
    assert n % t == 0, (n, want)
    return t


def kernel(x, attn_norm, w_in, gdn_conv_w, gdn_a_log, gdn_dt_bias, gdn_norm, lru_conv_w, lru_conv_b,
           lru_w_a, lru_b_a, lru_w_x, lru_b_x, lru_lambda, w_branch_gdn, w_branch_lru, w_out,
           mlp_norm, w_up, w_down, final_norm):
    batch, seq, d = x.shape
    depth = w_in.shape[0]
    heads = gdn_a_log.shape[1]
    gw = heads * LANES
    assert gdn_norm.shape[1] == LANES and gdn_conv_w.shape[2] == 3 * gw
    blocks = lru_w_a.shape[1]
    lw = blocks * LANES
    assert lru_w_a.shape[2] == LANES
    assert 2 * heads <= LANES and seq % (GDN_PACK * GDN_CHUNK) == 0
    assert gw == lw
    m = batch * seq

    o_z = 3 * gw
    o_a = o_z + gw
    o_xb = o_a + 2 * heads
    o_yb = o_xb + lw
    o_gg = o_yb + lw
    o_gl = o_gg + d
    assert w_in.shape[2] == o_gl + d
    xb_col = (o_a) // lw
    yb_col = xb_col + 1
    gg_col = (o_a + 2 * lw) // d
    gl_col = gg_col + 1
    assert o_a % lw == 0 and (o_a + 2 * lw) % d == 0

    tm_in = _tile(m, 1024)
    tn_in = _tile(o_gl + d - 2 * heads, 1024)
    ts_lru = _tile(seq, 512)
    tm_merge = _tile(m, 256)
    tm_mlp = _tile(m, 1024)
    tf_mlp = _tile(w_up.shape[2], 512)

    x2 = x.reshape(m, d)
    for l in range(depth):
        wl = w_in[l]
        w_main = jnp.concatenate([wl[:, :o_a], wl[:, o_xb:]], axis=1).astype(BF16)
        w_ab = jnp.pad(wl[:, o_a:o_xb], ((0, 0), (0, LANES - 2 * heads))).astype(BF16)
        proj, ab = _in_proj(x2, attn_norm[l].reshape(1, d), w_main, w_ab, tm_in, tn_in)

        o_gdn = _gdn(proj, ab, gdn_conv_w[l], _pad_row(gdn_a_log[l], LANES), _pad_row(gdn_dt_bias[l], LANES),
                     gdn_norm[l].reshape(1, LANES), batch, seq, heads)
        o_lru = _lru(proj, lru_conv_w[l], lru_conv_b[l].reshape(1, lw), lru_w_a[l].astype(BF16),
                     lru_b_a[l].reshape(1, lw), lru_w_x[l].astype(BF16), lru_b_x[l].reshape(1, lw),
                     lru_lambda[l].reshape(1, lw), batch, seq, blocks, xb_col, yb_col, ts_lru)
        x2 = _merge(x2, o_gdn, o_lru, proj, w_branch_gdn[l].astype(BF16), w_branch_lru[l].astype(BF16),
                    w_out[l].astype(BF16), gg_col, gl_col, tm_merge)
        last = l == depth - 1
        x2 = _mlp(x2, mlp_norm[l].reshape(1, d), w_up[l].astype(BF16), w_down[l].astype(BF16),
                  final_norm.reshape(1, d), last, tm_mlp, tf_mlp)
    if depth == 0:
        raise ValueError("depth must be positive")
    return x2.reshape(batch, seq, d)
```

```python
import functools

import jax
import jax.numpy as jnp
from jax import lax
from jax.experimental import pallas as pl
from jax.experimental.pallas import tpu as pltpu

F32 = jnp.float32
BF16 = jnp.bfloat16
HIGHEST = lax.Precision.HIGHEST

RMS_EPS = 1e-6
L2_EPS = 1e-6
LRU_C = 8.0
CONV_WIDTH = 4
GDN_CHUNK = 64
INV_BLOCK = 16
GDN_PACK = 4
CONV_ROWS = 256
LANES = 128
SUBLANES = 8
VMEM_LIMIT = 56 * 1024 * 1024


def _mm(a, b):
    return jnp.dot(a.astype(BF16), b.astype(BF16), preferred_element_type=F32)


def _mm_nt(a, b):
    return lax.dot_general(a.astype(BF16), b.astype(BF16), (((1,), (1,)), ((), ())),
                           preferred_element_type=F32)


def _mm_tn(a, b):
    return lax.dot_general(a.astype(BF16), b.astype(BF16), (((0,), (0,)), ((), ())),
                           preferred_element_type=F32)


def _mm_hi(a, b):
    return jnp.dot(a, b, preferred_element_type=F32, precision=HIGHEST)


def _softplus(x):
    return jnp.maximum(x, 0.0) + jnp.log1p(jnp.exp(-jnp.abs(x)))


def _silu(x):
    h = 0.5 * x
    return h + h * jnp.tanh(h)


def _gelu_tanh(x):
    c = 0.7978845608028654
    return 0.5 * x * (1.0 + jnp.tanh(c * (x + 0.044715 * (x * x * x))))


def _in_proj_kernel(x_ref, g_ref, w_ref, wab_ref, cw_ref, cb_ref, o_ref, oab_ref, h_ref, tail_ref, *,
                    tiles_per_seq, gdn_tiles, lru_tile):
    i = pl.program_id(0)
    j = pl.program_id(1)
    tm, tn = o_ref.shape

    @pl.when(j == 0)
    def _():
        x = x_ref[...]
        ms = jnp.mean(x * x, axis=-1, keepdims=True)
        h = (x * lax.rsqrt(ms + RMS_EPS) * g_ref[...]).astype(BF16)
        h_ref[...] = h
        oab_ref[...] = jnp.dot(h, wab_ref[...], preferred_element_type=F32)

        @pl.when(i == 0)
        def _():
            tail_ref[...] = jnp.zeros_like(tail_ref)

    rc = min(tm, CONV_ROWS)

    def conv_chunks(finish):
        col = pl.multiple_of(j * tn, tn)
        prev = jnp.where(i % tiles_per_seq == 0, 0.0, tail_ref[:, pl.ds(col, tn)])
        row8 = lax.broadcasted_iota(jnp.int32, (SUBLANES, tn), 0)
        for r0 in range(0, tm, rc):
            res = jnp.dot(h_ref[r0:r0 + rc, :], w_ref[...], preferred_element_type=F32)
            acc = res * cw_ref[CONV_WIDTH - 1:CONV_WIDTH, :]
            for s in range(1, CONV_WIDTH):
                rolled = pltpu.roll(res, s, axis=0)
                top = jnp.where(row8 < s, pltpu.roll(prev, s, axis=0), rolled[0:SUBLANES, :])
                shifted = jnp.concatenate([top, rolled[SUBLANES:, :]], axis=0)
                acc = acc + shifted * cw_ref[CONV_WIDTH - 1 - s:CONV_WIDTH - s, :]
            prev = res[rc - SUBLANES:rc, :]
            finish(acc, r0)
        tail_ref[:, pl.ds(col, tn)] = prev

    @pl.when(j < 3 * gdn_tiles)
    def _():
        is_v = j >= 2 * gdn_tiles
        q_scale = jnp.where(j < gdn_tiles, LANES ** -0.5, 1.0)

        def finish(acc, r0):
            y = _silu(acc)
            for g in range(tn // LANES):
                yg = y[:, g * LANES:(g + 1) * LANES]
                inv = lax.rsqrt(jnp.sum(yg * yg, axis=-1, keepdims=True) + L2_EPS)
                o_ref[r0:r0 + rc, g * LANES:(g + 1) * LANES] = (
                    yg * jnp.where(is_v, 1.0, inv * q_scale)).astype(o_ref.dtype)

        conv_chunks(finish)

    @pl.when(j == lru_tile)
    def _():
        def finish(acc, r0):
            o_ref[r0:r0 + rc, :] = (acc + cb_ref[...]).astype(o_ref.dtype)

        conv_chunks(finish)

    @pl.when((j >= 3 * gdn_tiles) & (j != lru_tile))
    def _():
        o_ref[...] = jnp.dot(h_ref[...], w_ref[...], preferred_element_type=F32).astype(o_ref.dtype)


def _in_proj(x2, gain, w_main, w_ab, conv_w, conv_b, layer, tm, tn, seq, gdn_tiles, lru_tile):
    m, d = x2.shape
    n = w_main.shape[2]
    kern = functools.partial(_in_proj_kernel, tiles_per_seq=seq // tm, gdn_tiles=gdn_tiles, lru_tile=lru_tile)
    return pl.pallas_call(
        kern,
        grid=(m // tm, n // tn),
        in_specs=[
            pl.BlockSpec((tm, d), lambda i, j: (i, 0)),
            pl.BlockSpec((None, 1, d), lambda i, j: (layer, 0, 0)),
            pl.BlockSpec((None, d, tn), lambda i, j: (layer, 0, j)),
            pl.BlockSpec((None, d, LANES), lambda i, j: (layer, 0, 0)),
            pl.BlockSpec((None, CONV_WIDTH, tn), lambda i, j: (layer, 0, j)),
            pl.BlockSpec((None, 1, tn), lambda i, j: (layer, 0, j)),
        ],
        out_specs=[
            pl.BlockSpec((tm, tn), lambda i, j: (i, j)),
            pl.BlockSpec((tm, LANES), lambda i, j: (i, 0)),
        ],
        out_shape=[
            jax.ShapeDtypeStruct((m, n), BF16),
            jax.ShapeDtypeStruct((m, LANES), F32),
        ],
        scratch_shapes=[pltpu.VMEM((tm, d), BF16), pltpu.VMEM((SUBLANES, n), F32)],
        compiler_params=pltpu.CompilerParams(
            dimension_semantics=("arbitrary", "arbitrary"), vmem_limit_bytes=VMEM_LIMIT),
        name="in_proj",
    )(x2, gain, w_main, w_ab, conv_w, conv_b)


def _gdn_kernel(qkv_ref, z_ref, ab_ref, alog_ref, dtb_ref, gain_ref, o_ref,
                state, qk_s, rhs_s, u_s, w_s, qd_s, kd_s, *, heads):
    hd = LANES
    width = heads * hd
    c = GDN_CHUNK
    ts = GDN_PACK * c
    hs = range(heads)

    @pl.when(pl.program_id(1) == 0)
    def _():
        state[...] = jnp.zeros_like(state)

    ab = ab_ref[...]
    g_all = -jnp.exp(alog_ref[...]) * _softplus(ab + dtb_ref[...])
    beta_all = jax.nn.sigmoid(ab)

    r2 = lax.broadcasted_iota(jnp.int32, (ts, ts), 0)
    c2 = lax.broadcasted_iota(jnp.int32, (ts, ts), 1)
    same_chunk = (r2 & -c) == (c2 & -c)
    causal_bd = same_chunk & (r2 >= c2)
    strict_bd = same_chunk & (r2 > c2)
    bd_mask = jnp.where(same_chunk, 1.0, 0.0).astype(BF16)
    rp = lax.broadcasted_iota(jnp.int32, (c, ts), 0)
    cp = lax.broadcasted_iota(jnp.int32, (c, ts), 1) & (c - 1)
    eye_p = jnp.where(rp == cp, 1.0, 0.0)
    in_diag_block = (rp & -INV_BLOCK) == (cp & -INV_BLOCK)

    def blockdiag(y):
        yb = y.astype(BF16)
        return jnp.concatenate([yb] * GDN_PACK, axis=0) * bd_mask

    def bmm(x, bd):
        return jnp.dot(x.astype(BF16), bd, preferred_element_type=F32)

    sums = _mm_hi(jnp.concatenate([jnp.where(causal_bd, 1.0, 0.0), jnp.where(same_chunk, 1.0, 0.0)], axis=0), g_all)
    gcum = sums[0:ts]
    glast = sums[ts:2 * ts]
    gcum_t = gcum.T

    prods = []
    for h in hs:
        qb = qkv_ref[:, h * hd:(h + 1) * hd]
        kb16 = qkv_ref[:, width + h * hd:width + (h + 1) * hd]
        q = qb.astype(F32)
        k = kb16.astype(F32)
        v = qkv_ref[:, 2 * width + h * hd:2 * width + (h + 1) * hd].astype(F32)
        gcol = gcum[:, h:h + 1]
        egc = jnp.exp(gcol)
        beta = beta_all[:, heads + h:heads + h + 1]
        kb = k * beta
        prods.append(lax.dot_general(jnp.concatenate([qb, kb.astype(BF16)], axis=0), kb16,
                                     (((1,), (1,)), ((), ())), preferred_element_type=F32))
        rhs_s[h] = jnp.concatenate([v * beta, kb * egc], axis=1).astype(BF16)
        qd_s[h] = (q * egc).astype(BF16)
        kd_s[h] = (k * jnp.exp(glast[:, h:h + 1] - gcol)).astype(BF16)
    lps = []
    for h in hs:
        gcol = gcum[:, h:h + 1]
        grow = gcum_t[h:h + 1, :]
        dmat = jnp.where(causal_bd, jnp.exp(jnp.where(causal_bd, gcol - grow, 0.0)), 0.0)
        qk_s[h] = (prods[h][0:ts] * dmat).astype(BF16)
        lbd = jnp.where(strict_bd, prods[h][ts:2 * ts] * dmat, 0.0)
        lp = lbd[0:c]
        for i in range(1, GDN_PACK):
            lp = lp + lbd[i * c:(i + 1) * c]
        lps.append(lp)

    ld = [jnp.where(in_diag_block, lp, 0.0) for lp in lps]
    lo = [lp - d for lp, d in zip(lps, ld)]
    dinv = [eye_p - d for d in ld]
    pw = [bmm(d, blockdiag(d)) for d in ld]
    span = 2
    while span < INV_BLOCK:
        last = 2 * span >= INV_BLOCK
        res = [bmm(d if last else jnp.concatenate([p, d], axis=0), blockdiag(p)) for p, d in zip(pw, dinv)]
        if last:
            dinv = [d + r for d, r in zip(dinv, res)]
        else:
            pw = [r[0:c] for r in res]
            dinv = [d + r[c:2 * c] for d, r in zip(dinv, res)]
        span *= 2
    n = [bmm(d, blockdiag(l)) for d, l in zip(dinv, lo)]
    acc = [eye_p - x for x in n]
    pw = n
    span = 2
    while span < c // INV_BLOCK:
        pw = [bmm(p, blockdiag(p)) for p in pw]
        acc = [a + bmm(a, blockdiag(p)) for a, p in zip(acc, pw)]
        span *= 2
    t0 = [bmm(a, blockdiag(d)) for a, d in zip(acc, dinv)]
    t_hi = [x.astype(BF16) for x in t0]
    t_lo = [x - xh.astype(F32) for x, xh in zip(t0, t_hi)]
    a_mat = [eye_p + lp for lp in lps]
    a_hi = [x.astype(BF16) for x in a_mat]
    a_lo = [(x - xh.astype(F32)).astype(BF16) for x, xh in zip(a_mat, a_hi)]
    bd_hi = [blockdiag(x) for x in t_hi]
    r1 = [jnp.dot(jnp.concatenate([ah, al], axis=0), bd, preferred_element_type=F32)
          for ah, al, bd in zip(a_hi, a_lo, bd_hi)]
    r2_ = [jnp.dot(ah, blockdiag(tl), preferred_element_type=F32) for ah, tl in zip(a_hi, t_lo)]
    resid = [eye_p - (x[0:c] + x[c:2 * c] + y) for x, y in zip(r1, r2_)]
    tinv = [x + jnp.dot(xh, blockdiag(r), preferred_element_type=F32) for x, xh, r in zip(t0, t_hi, resid)]

    for h in hs:
        uw = jnp.dot(blockdiag(tinv[h]), rhs_s[h], preferred_element_type=F32)
        u_s[h] = uw[:, 0:hd]
        w_s[h] = uw[:, hd:2 * hd].astype(BF16)

    gain = gain_ref[...]
    zero_v = jnp.zeros((c, hd), BF16)
    for ci in range(GDN_PACK):
        r0 = ci * c
        slab = (r0 // hd) * hd
        wss = [_mm(jnp.concatenate([w_s[h, r0:r0 + c, :], qd_s[h, r0:r0 + c, :]], axis=0), state[h]) for h in hs]
        for h in hs:
            v_new = u_s[h, r0:r0 + c, :] - wss[h][0:c]
            v_new_b = v_new.astype(BF16)
            pieces = [zero_v] * (hd // c)
            pieces[(r0 - slab) // c] = v_new_b
            out = wss[h][c:2 * c] + jnp.dot(qk_s[h, r0:r0 + c, slab:slab + hd], jnp.concatenate(pieces, axis=0),
                                            preferred_element_type=F32)
            state[h] = (state[h] * jnp.exp(glast[r0:r0 + 1, h:h + 1])
                        + _mm_tn(kd_s[h, r0:r0 + c, :], v_new_b))
            o = out * lax.rsqrt(jnp.mean(out * out, axis=-1, keepdims=True) + RMS_EPS) * gain
            zz = z_ref[r0:r0 + c, h * hd:(h + 1) * hd].astype(F32)
            o_ref[r0:r0 + c, h * hd:(h + 1) * hd] = (o * _silu(zz)).astype(o_ref.dtype)


def _gdn(proj, ab, alog_row, dtb_row, gain_row, layer, batch, seq, heads):
    width = heads * LANES
    ts = GDN_PACK * GDN_CHUNK
    nt = seq // ts
    kern = functools.partial(_gdn_kernel, heads=heads)
    return pl.pallas_call(
        kern,
        grid=(batch, nt),
        in_specs=[
            pl.BlockSpec((ts, 3 * width), lambda b, t: (b * nt + t, 0)),
            pl.BlockSpec((ts, width), lambda b, t: (b * nt + t, 3)),
            pl.BlockSpec((ts, LANES), lambda b, t: (b * nt + t, 0)),
            pl.BlockSpec((None, 1, LANES), lambda b, t: (layer, 0, 0)),
            pl.BlockSpec((None, 1, LANES), lambda b, t: (layer, 0, 0)),
            pl.BlockSpec((None, 1, LANES), lambda b, t: (layer, 0, 0)),
        ],
        out_specs=pl.BlockSpec((ts, width), lambda b, t: (b * nt + t, 0)),
        out_shape=jax.ShapeDtypeStruct((batch * seq, width), BF16),
        scratch_shapes=[
            pltpu.VMEM((heads, LANES, LANES), F32),
            pltpu.VMEM((heads, ts, ts), BF16),
            pltpu.VMEM((heads, ts, 2 * LANES), BF16),
            pltpu.VMEM((heads, ts, LANES), F32),
            pltpu.VMEM((heads, ts, LANES), BF16),
            pltpu.VMEM((heads, ts, LANES), BF16),
            pltpu.VMEM((heads, ts, LANES), BF16),
        ],
        compiler_params=pltpu.CompilerParams(
            dimension_semantics=("arbitrary", "arbitrary"), vmem_limit_bytes=VMEM_LIMIT),
        name="gdn",
    )(proj, proj, ab, alog_row, dtb_row, gain_row)


def _lru_kernel(xc_ref, yb_ref, wa_ref, ba_ref, wx_ref, bx_ref, lam_ref, o_ref,
                abuf, bbuf, hcar, *, blocks, ts):
    bd = LANES
    width = blocks * bd

    @pl.when(pl.program_id(1) == 0)
    def _():
        hcar[...] = jnp.zeros_like(hcar)

    for n in range(blocks):
        lo = n * bd
        xcb = xc_ref[:, lo:lo + bd]
        xc = xcb.astype(F32)
        r = jax.nn.sigmoid(jnp.dot(xcb, wa_ref[n], preferred_element_type=F32) + ba_ref[0:1, lo:lo + bd])
        i = jax.nn.sigmoid(jnp.dot(xcb, wx_ref[n], preferred_element_type=F32) + bx_ref[0:1, lo:lo + bd])
        log_a = (-LRU_C) * r * _softplus(-lam_ref[0:1, lo:lo + bd])
        a = jnp.exp(log_a)
        one_minus_a2 = -jnp.tanh(log_a) * (1.0 + a * a)
        abuf[:, lo:lo + bd] = a
        bbuf[:, lo:lo + bd] = jnp.sqrt(one_minus_a2) * (i * xc)

    rowi = lax.broadcasted_iota(jnp.int32, (SUBLANES, width), 0)

    def group(gi, hprev):
        r0 = pl.multiple_of(gi * SUBLANES, SUBLANES)
        a = abuf[pl.ds(r0, SUBLANES), :]
        b = bbuf[pl.ds(r0, SUBLANES), :]
        s = 1
        while s < SUBLANES:
            keep = rowi >= s
            a_sh = jnp.where(keep, pltpu.roll(a, s, axis=0), 1.0)
            b_sh = jnp.where(keep, pltpu.roll(b, s, axis=0), 0.0)
            b = a * b_sh + b
            a = a * a_sh
            s *= 2
        bbuf[pl.ds(r0, SUBLANES), :] = a * hprev + b
        return a[SUBLANES - 1:SUBLANES, :] * hprev + b[SUBLANES - 1:SUBLANES, :]

    hcar[...] = lax.fori_loop(0, ts // SUBLANES, group, hcar[...], unroll=4)
    o_ref[...] = (bbuf[...] * _gelu_tanh(yb_ref[...].astype(F32))).astype(o_ref.dtype)


def _lru(proj, w_a, b_a, w_x, b_x, lam, layer, batch, seq, blocks, xb_col, yb_col, ts):
    width = blocks * LANES
    nt = seq // ts
    kern = functools.partial(_lru_kernel, blocks=blocks, ts=ts)
    row = lambda b, t: (layer, 0, 0)
    mat = lambda b, t: (layer, 0, 0, 0)
    return pl.pallas_call(
        kern,
        grid=(batch, nt),
        in_specs=[
            pl.BlockSpec((ts, width), lambda b, t: (b * nt + t, xb_col)),
            pl.BlockSpec((ts, width), lambda b, t: (b * nt + t, yb_col)),
            pl.BlockSpec((None, blocks, LANES, LANES), mat),
            pl.BlockSpec((None, 1, width), row),
            pl.BlockSpec((None, blocks, LANES, LANES), mat),
            pl.BlockSpec((None, 1, width), row),
            pl.BlockSpec((None, 1, width), row),
        ],
        out_specs=pl.BlockSpec((ts, width), lambda b, t: (b * nt + t, 0)),
        out_shape=jax.ShapeDtypeStruct((batch * seq, width), BF16),
        scratch_shapes=[
            pltpu.VMEM((ts, width), F32),
            pltpu.VMEM((ts, width), F32),
            pltpu.VMEM((1, width), F32),
        ],
        compiler_params=pltpu.CompilerParams(
            dimension_semantics=("arbitrary", "arbitrary"), vmem_limit_bytes=VMEM_LIMIT),
        name="lru",
    )(proj, proj, w_a, b_a, w_x, b_x, lam)


def _merge_kernel(x_ref, og_ref, ol_ref, gg_ref, gl_ref, wg_ref, wl_ref, wo_ref, o_ref):
    pg = jnp.dot(og_ref[...], wg_ref[...], preferred_element_type=F32)
    pb = jnp.dot(ol_ref[...], wl_ref[...], preferred_element_type=F32)
    merged = (jax.nn.sigmoid(gg_ref[...].astype(F32)) * pg
              + jax.nn.sigmoid(gl_ref[...].astype(F32)) * pb)
    o_ref[...] = x_ref[...] + jnp.dot(merged.astype(BF16), wo_ref[...], preferred_element_type=F32)


def _merge(x2, o_gdn, o_lru, proj, w_bg, w_bl, w_out, layer, gg_col, gl_col, tm):
    m, d = x2.shape
    wg = o_gdn.shape[1]
    wl = o_lru.shape[1]
    const = lambda i: (layer, 0, 0)
    return pl.pallas_call(
        _merge_kernel,
        grid=(m // tm,),
        in_specs=[
            pl.BlockSpec((tm, d), lambda i: (i, 0)),
            pl.BlockSpec((tm, wg), lambda i: (i, 0)),
            pl.BlockSpec((tm, wl), lambda i: (i, 0)),
            pl.BlockSpec((tm, d), lambda i: (i, gg_col)),
            pl.BlockSpec((tm, d), lambda i: (i, gl_col)),
            pl.BlockSpec((None, wg, d), const, pipeline_mode=pl.Buffered(1)),
            pl.BlockSpec((None, wl, d), const, pipeline_mode=pl.Buffered(1)),
            pl.BlockSpec((None, d, d), const, pipeline_mode=pl.Buffered(1)),
        ],
        out_specs=pl.BlockSpec((tm, d), lambda i: (i, 0)),
        out_shape=jax.ShapeDtypeStruct((m, d), F32),
        compiler_params=pltpu.CompilerParams(
            dimension_semantics=("arbitrary",), vmem_limit_bytes=VMEM_LIMIT),
        name="merge",
    )(x2, o_gdn, o_lru, proj, proj, w_bg, w_bl, w_out)


def _mlp_kernel(x_ref, g_ref, wu_ref, wd_ref, fg_ref, o_ref, h_ref, *, final_norm):
    f = pl.program_id(1)

    @pl.when(f == 0)
    def _():
        x = x_ref[...]
        ms = jnp.mean(x * x, axis=-1, keepdims=True)
        h_ref[...] = (x * lax.rsqrt(ms + RMS_EPS) * g_ref[...]).astype(BF16)
        o_ref[...] = x

    u = jnp.dot(h_ref[...], wu_ref[...], preferred_element_type=F32)
    u = jnp.maximum(u, 0.0)
    o_ref[...] += jnp.dot((u * u).astype(BF16), wd_ref[...], preferred_element_type=F32)

    if final_norm:
        @pl.when(f == pl.num_programs(1) - 1)
        def _():
            y = o_ref[...]
            ms = jnp.mean(y * y, axis=-1, keepdims=True)
            o_ref[...] = y * lax.rsqrt(ms + RMS_EPS) * fg_ref[...]


def _mlp(x2, gain, w_up, w_down, final_gain, layer, final_norm, tm, tf):
    m, d = x2.shape
    ff = w_up.shape[2]
    kern = functools.partial(_mlp_kernel, final_norm=final_norm)
    return pl.pallas_call(
        kern,
        grid=(m // tm, ff // tf),
        in_specs=[
            pl.BlockSpec((tm, d), lambda i, f: (i, 0)),
            pl.BlockSpec((None, 1, d), lambda i, f: (layer, 0, 0)),
            pl.BlockSpec((None, d, tf), lambda i, f: (layer, 0, f)),
            pl.BlockSpec((None, tf, d), lambda i, f: (layer, f, 0)),
            pl.BlockSpec((1, d), lambda i, f: (0, 0)),
        ],
        out_specs=pl.BlockSpec((tm, d), lambda i, f: (i, 0)),
        out_shape=jax.ShapeDtypeStruct((m, d), F32),
        scratch_shapes=[pltpu.VMEM((tm, d), BF16)],
        compiler_params=pltpu.CompilerParams(
            dimension_semantics=("arbitrary", "arbitrary"), vmem_limit_bytes=VMEM_LIMIT),
        name="mlp",
    )(x2, gain, w_up, w_down, final_gain)


def _tile(n, want):
    t = min(want, n)
    assert n % t == 0, (n, want)
    return t


def kernel(x, attn_norm, w_in, gdn_conv_w, gdn_a_log, gdn_dt_bias, gdn_norm, lru_conv_w, lru_conv_b,
           lru_w_a, lru_b_a, lru_w_x, lru_b_x, lru_lambda, w_branch_gdn, w_branch_lru, w_out,
           mlp_norm, w_up, w_down, final_norm):
    batch, seq, d = x.shape
    depth = w_in.shape[0]
    heads = gdn_a_log.shape[1]
    gw = heads * LANES
    assert gdn_norm.shape[1] == LANES and gdn_conv_w.shape[2] == 3 * gw
    blocks = lru_w_a.shape[1]
    lw = blocks * LANES
    assert lru_w_a.shape[2] == LANES
    assert 2 * heads <= LANES and seq % (GDN_PACK * GDN_CHUNK) == 0
    assert gw == lw
    m = batch * seq

    o_z = 3 * gw
    o_a = o_z + gw
    o_xb = o_a + 2 * heads
    o_yb = o_xb + lw
    o_gg = o_yb + lw
    o_gl = o_gg + d
    assert w_in.shape[2] == o_gl + d
    xb_col = (o_a) // lw
    yb_col = xb_col + 1
    gg_col = (o_a + 2 * lw) // d
    gl_col = gg_col + 1
    assert o_a % lw == 0 and (o_a + 2 * lw) % d == 0

    n_main = o_gl + d - 2 * heads
    tm_in = _tile(seq, 1024)
    tn_in = gw
    assert n_main % tn_in == 0
    ts_lru = _tile(seq, 512)
    tm_merge = _tile(m, 256)
    tm_mlp = _tile(m, 1024)
    tf_mlp = _tile(w_up.shape[2], 512)

    w_main = jnp.concatenate([w_in[:, :, :o_a], w_in[:, :, o_xb:]], axis=2).astype(BF16)
    w_ab = jnp.pad(w_in[:, :, o_a:o_xb], ((0, 0), (0, 0), (0, LANES - 2 * heads))).astype(BF16)
    conv_w = jnp.concatenate([gdn_conv_w.astype(F32), jnp.zeros((depth, CONV_WIDTH, gw), F32),
                              lru_conv_w.astype(F32), jnp.zeros((depth, CONV_WIDTH, n_main - o_a - lw), F32)], axis=2)
    conv_b = jnp.pad(lru_conv_b.astype(F32), ((0, 0), (o_a, n_main - o_a - lw)))[:, None, :]
    pad_h = ((0, 0), (0, LANES - heads))
    alog_rows = jnp.pad(gdn_a_log.astype(F32), pad_h)[:, None, :]
    dtb_rows = jnp.pad(gdn_dt_bias.astype(F32), pad_h)[:, None, :]
    w_a_b, w_x_b = lru_w_a.astype(BF16), lru_w_x.astype(BF16)
    w_bg_b, w_bl_b, w_out_b = w_branch_gdn.astype(BF16), w_branch_lru.astype(BF16), w_out.astype(BF16)
    w_up_b, w_down_b = w_up.astype(BF16), w_down.astype(BF16)
    row3 = lambda v: v[:, None, :]

    x2 = x.reshape(m, d)
    for l in range(depth):
        proj, ab = _in_proj(x2, row3(attn_norm), w_main, w_ab, conv_w, conv_b, l, tm_in, tn_in, seq,
                            gw // tn_in, o_a // tn_in)
        o_gdn = _gdn(proj, ab, alog_rows, dtb_rows, row3(gdn_norm), l, batch, seq, heads)
        o_lru = _lru(proj, w_a_b, row3(lru_b_a), w_x_b, row3(lru_b_x),
                     row3(lru_lambda), l, batch, seq, blocks, xb_col, yb_col, ts_lru)
        x2 = _merge(x2, o_gdn, o_lru, proj, w_bg_b, w_bl_b, w_out_b, l, gg_col, gl_col, tm_merge)
        x2 = _mlp(x2, row3(mlp_norm), w_up_b, w_down_b, final_norm.reshape(1, d), l, l == depth - 1,
                  tm_mlp, tf_mlp)
    if depth == 0:
        raise ValueError("depth must be positive")
    return x2.reshape(batch, seq, d)
```

```python
import functools

import jax
import jax.numpy as jnp
from jax import lax
from jax.experimental import pallas as pl
from jax.experimental.pallas import tpu as pltpu

F32 = jnp.float32
BF16 = jnp.bfloat16
HIGHEST = lax.Precision.HIGHEST

RMS_EPS = 1e-6
L2_EPS = 1e-6
LRU_C = 8.0
CONV_WIDTH = 4
GDN_CHUNK = 64
INV_BLOCK = 16
GDN_PACK = 4
CONV_COLS = 256
LANES = 128
SUBLANES = 8
BF16_ROWS = 16
VMEM_LIMIT = 56 * 1024 * 1024


def _mm(a, b):
    return jnp.dot(a.astype(BF16), b.astype(BF16), preferred_element_type=F32)


def _mm_nt(a, b):
    return lax.dot_general(a.astype(BF16), b.astype(BF16), (((1,), (1,)), ((), ())),
                           preferred_element_type=F32)


def _mm_tn(a, b):
    return lax.dot_general(a.astype(BF16), b.astype(BF16), (((0,), (0,)), ((), ())),
                           preferred_element_type=F32)


def _mm_hi(a, b):
    return jnp.dot(a, b, preferred_element_type=F32, precision=HIGHEST)


def _softplus(x):
    return jnp.maximum(x, 0.0) + jnp.log1p(jnp.exp(-jnp.abs(x)))


def _silu(x):
    h = 0.5 * x
    return h + h * jnp.tanh(h)


def _gelu_tanh(x):
    c = 0.7978845608028654
    return 0.5 * x * (1.0 + jnp.tanh(c * (x + 0.044715 * (x * x * x))))


def _cast_plan(w, steps):
    rows = w.shape[1]
    units = rows // BF16_ROWS
    assert rows % BF16_ROWS == 0
    nb = max(k for k in range(1, min(units, steps) + 1) if units % k == 0)
    return nb, rows // nb


def _cast_specs(w, layer, steps, step_of):
    nb, rb = _cast_plan(w, steps)
    cols = w.shape[2]
    blk = lambda *ids: jnp.minimum(step_of(*ids), nb - 1)
    return (pl.BlockSpec((None, rb, cols), lambda *ids: (layer, blk(*ids), 0)),
            pl.BlockSpec((rb, cols), lambda *ids: (blk(*ids), 0)),
            jax.ShapeDtypeStruct((w.shape[1], cols), BF16))


def _cast_step(step, srcs, dsts, nbs):
    for src, dst, nb in zip(srcs, dsts, nbs):
        @pl.when(step < nb)
        def _(src=src, dst=dst):
            dst[...] = src[...].astype(dst.dtype)


def _in_proj_kernel(x_ref, g_ref, w_ref, wab_ref, cw_ref, cb_ref, c0_ref, c1_ref, c2_ref,
                    o_ref, oab_ref, d0_ref, d1_ref, d2_ref, h_ref, tail_ref, *,
                    tiles_per_seq, lru_step, cast_blocks):
    i = pl.program_id(0)
    j = pl.program_id(1)
    tm, tn = o_ref.shape
    half = tn // 2
    _cast_step(i * pl.num_programs(1) + j, (c0_ref, c1_ref, c2_ref), (d0_ref, d1_ref, d2_ref), cast_blocks)

    @pl.when(j == 0)
    def _():
        x = x_ref[...]
        ms = jnp.mean(x * x, axis=-1, keepdims=True)
        h = (x * lax.rsqrt(ms + RMS_EPS) * g_ref[...]).astype(BF16)
        h_ref[...] = h
        oab_ref[...] = jnp.dot(h, wab_ref[...], preferred_element_type=F32)

        @pl.when(i == 0)
        def _():
            tail_ref[...] = jnp.zeros_like(tail_ref)

    cc = min(half, CONV_COLS)
    row8 = lax.broadcasted_iota(jnp.int32, (SUBLANES, cc), 0)

    def shift_rows(val, prev8, s):
        rolled = pltpu.roll(val, s, axis=0)
        top = jnp.where(row8 < s, pltpu.roll(prev8, s, axis=0), rolled[0:SUBLANES, :])
        return jnp.concatenate([top, rolled[SUBLANES:, :]], axis=0)

    def conv_chunks(finish):
        first = i % tiles_per_seq == 0
        nb = half // cc
        for ci in range(nb):
            c0 = ci * cc
            col = pl.multiple_of(j * half, half) + c0
            prev = jnp.where(first, 0.0, tail_ref[:, pl.ds(col, cc)])
            res = jnp.dot(h_ref[...], w_ref[:, c0:c0 + cc], preferred_element_type=F32)
            tail_ref[:, pl.ds(col, cc)] = res[tm - SUBLANES:tm, :]
            b0 = half + c0
            o_ref[:, b0:b0 + cc] = jnp.dot(h_ref[...], w_ref[:, b0:b0 + cc],
                                           preferred_element_type=F32).astype(o_ref.dtype)
            taps = [cw_ref[k:k + 1, c0:c0 + cc] for k in range(CONV_WIDTH)]
            s1 = shift_rows(res, prev, 1)
            u = res * taps[3] + s1 * taps[2]
            v = res * taps[1] + s1 * taps[0]
            prev_v = prev * taps[1] + pltpu.roll(prev, 1, axis=0) * taps[0]
            finish(u + shift_rows(v, prev_v, 2), c0)

    @pl.when(j < 3)
    def _():
        is_v = j == 2
        q_scale = jnp.where(j == 0, LANES ** -0.5, 1.0)

        def finish(acc, c0):
            y = _silu(acc)
            for g in range(cc // LANES):
                yg = y[:, g * LANES:(g + 1) * LANES]
                inv = lax.rsqrt(jnp.sum(yg * yg, axis=-1, keepdims=True) + L2_EPS)
                o_ref[:, c0 + g * LANES:c0 + (g + 1) * LANES] = (
                    yg * jnp.where(is_v, 1.0, inv * q_scale)).astype(o_ref.dtype)

        conv_chunks(finish)

    @pl.when(j == lru_step)
    def _():
        def finish(acc, c0):
            o_ref[:, c0:c0 + cc] = (acc + cb_ref[:, c0:c0 + cc]).astype(o_ref.dtype)

        conv_chunks(finish)

    @pl.when((j >= 3) & (j != lru_step))
    def _():
        o_ref[...] = jnp.dot(h_ref[...], w_ref[...], preferred_element_type=F32).astype(o_ref.dtype)


def _in_proj(x2, gain, w_main, w_ab, conv_w, conv_b, casts, layer, tm, tn, seq, lru_step):
    m, d = x2.shape
    n = w_main.shape[2]
    half = tn // 2
    nj = n // tn
    steps = (m // tm) * nj
    cast = [_cast_specs(w, layer, steps, lambda i, j: i * nj + j) for w in casts]
    kern = functools.partial(_in_proj_kernel, tiles_per_seq=seq // tm, lru_step=lru_step,
                             cast_blocks=tuple(_cast_plan(w, steps)[0] for w in casts))
    return pl.pallas_call(
        kern,
        grid=(m // tm, nj),
        in_specs=[
            pl.BlockSpec((tm, d), lambda i, j: (i, 0)),
            pl.BlockSpec((None, 1, d), lambda i, j: (layer, 0, 0)),
            pl.BlockSpec((None, d, tn), lambda i, j: (layer, 0, j)),
            pl.BlockSpec((None, d, LANES), lambda i, j: (layer, 0, 0)),
            pl.BlockSpec((None, CONV_WIDTH, half), lambda i, j: (layer, 0, j)),
            pl.BlockSpec((None, 1, half), lambda i, j: (layer, 0, j)),
        ] + [c[0] for c in cast],
        out_specs=[
            pl.BlockSpec((tm, tn), lambda i, j: (i, j)),
            pl.BlockSpec((tm, LANES), lambda i, j: (i, 0)),
        ] + [c[1] for c in cast],
        out_shape=[
            jax.ShapeDtypeStruct((m, n), BF16),
            jax.ShapeDtypeStruct((m, LANES), F32),
        ] + [c[2] for c in cast],
        scratch_shapes=[pltpu.VMEM((tm, d), BF16), pltpu.VMEM((SUBLANES, n // 2), F32)],
        compiler_params=pltpu.CompilerParams(
            dimension_semantics=("arbitrary", "arbitrary"), vmem_limit_bytes=VMEM_LIMIT),
        name="in_proj",
    )(x2, gain, w_main, w_ab, conv_w, conv_b, *casts)


def _gdn_kernel(q_ref, k_ref, v_ref, z_ref, ab_ref, alog_ref, dtb_ref, gain_ref, o_ref,
                state, qk_s, rhs_s, u_s, w_s, qd_s, kd_s, *, heads):
    hd = LANES
    width = heads * hd
    c = GDN_CHUNK
    ts = GDN_PACK * c
    hs = range(heads)

    @pl.when(pl.program_id(1) == 0)
    def _():
        state[...] = jnp.zeros_like(state)

    ab = ab_ref[...]
    g_all = -jnp.exp(alog_ref[...]) * _softplus(ab + dtb_ref[...])
    beta_all = jax.nn.sigmoid(ab)

    r2 = lax.broadcasted_iota(jnp.int32, (ts, ts), 0)
    c2 = lax.broadcasted_iota(jnp.int32, (ts, ts), 1)
    same_chunk = (r2 & -c) == (c2 & -c)
    causal_bd = same_chunk & (r2 >= c2)
    strict_bd = same_chunk & (r2 > c2)
    bd_mask = jnp.where(same_chunk, 1.0, 0.0).astype(BF16)
    rp = lax.broadcasted_iota(jnp.int32, (c, ts), 0)
    cp = lax.broadcasted_iota(jnp.int32, (c, ts), 1) & (c - 1)
    eye_p = jnp.where(rp == cp, 1.0, 0.0)
    in_diag_block = (rp & -INV_BLOCK) == (cp & -INV_BLOCK)

    def blockdiag(y):
        yb = y.astype(BF16)
        return jnp.concatenate([yb] * GDN_PACK, axis=0) * bd_mask

    def bmm(x, bd):
        return jnp.dot(x.astype(BF16), bd, preferred_element_type=F32)

    sums = _mm_hi(jnp.concatenate([jnp.where(causal_bd, 1.0, 0.0), jnp.where(same_chunk, 1.0, 0.0)], axis=0), g_all)
    gcum = sums[0:ts]
    glast = sums[ts:2 * ts]
    gcum_t = gcum.T

    prods = []
    for h in hs:
        qb = q_ref[:, h * hd:(h + 1) * hd]
        kb16 = k_ref[:, h * hd:(h + 1) * hd]
        q = qb.astype(F32)
        k = kb16.astype(F32)
        v = v_ref[:, h * hd:(h + 1) * hd].astype(F32)
        gcol = gcum[:, h:h + 1]
        egc = jnp.exp(gcol)
        beta = beta_all[:, heads + h:heads + h + 1]
        kb = k * beta
        prods.append(lax.dot_general(jnp.concatenate([qb, kb.astype(BF16)], axis=0), kb16,
                                     (((1,), (1,)), ((), ())), preferred_element_type=F32))
        rhs_s[h] = jnp.concatenate([v * beta, kb * egc], axis=1).astype(BF16)
        qd_s[h] = (q * egc).astype(BF16)
        kd_s[h] = (k * jnp.exp(glast[:, h:h + 1] - gcol)).astype(BF16)
    lps = []
    for h in hs:
        gcol = gcum[:, h:h + 1]
        grow = gcum_t[h:h + 1, :]
        dmat = jnp.where(causal_bd, jnp.exp(jnp.where(causal_bd, gcol - grow, 0.0)), 0.0)
        qk_s[h] = (prods[h][0:ts] * dmat).astype(BF16)
        lbd = jnp.where(strict_bd, prods[h][ts:2 * ts] * dmat, 0.0)
        lp = lbd[0:c]
        for i in range(1, GDN_PACK):
            lp = lp + lbd[i * c:(i + 1) * c]
        lps.append(lp)

    ld = [jnp.where(in_diag_block, lp, 0.0) for lp in lps]
    lo = [lp - d for lp, d in zip(lps, ld)]
    dinv = [eye_p - d for d in ld]
    pw = [bmm(d, blockdiag(d)) for d in ld]
    span = 2
    while span < INV_BLOCK:
        last = 2 * span >= INV_BLOCK
        res = [bmm(d if last else jnp.concatenate([p, d], axis=0), blockdiag(p)) for p, d in zip(pw, dinv)]
        if last:
            dinv = [d + r for d, r in zip(dinv, res)]
        else:
            pw = [r[0:c] for r in res]
            dinv = [d + r[c:2 * c] for d, r in zip(dinv, res)]
        span *= 2
    n = [bmm(d, blockdiag(l)) for d, l in zip(dinv, lo)]
    acc = [eye_p - x for x in n]
    pw = n
    span = 2
    while span < c // INV_BLOCK:
        pw = [bmm(p, blockdiag(p)) for p in pw]
        acc = [a + bmm(a, blockdiag(p)) for a, p in zip(acc, pw)]
        span *= 2
    t0 = [bmm(a, blockdiag(d)) for a, d in zip(acc, dinv)]
    t_hi = [x.astype(BF16) for x in t0]
    t_lo = [x - xh.astype(F32) for x, xh in zip(t0, t_hi)]
    a_mat = [eye_p + lp for lp in lps]
    a_hi = [x.astype(BF16) for x in a_mat]
    a_lo = [(x - xh.astype(F32)).astype(BF16) for x, xh in zip(a_mat, a_hi)]
    bd_hi = [blockdiag(x) for x in t_hi]
    r1 = [jnp.dot(jnp.concatenate([ah, al], axis=0), bd, preferred_element_type=F32)
          for ah, al, bd in zip(a_hi, a_lo, bd_hi)]
    r2_ = [jnp.dot(ah, blockdiag(tl), preferred_element_type=F32) for ah, tl in zip(a_hi, t_lo)]
    resid = [eye_p - (x[0:c] + x[c:2 * c] + y) for x, y in zip(r1, r2_)]
    tinv = [x + jnp.dot(xh, blockdiag(r), preferred_element_type=F32) for x, xh, r in zip(t0, t_hi, resid)]

    for h in hs:
        uw = jnp.dot(blockdiag(tinv[h]), rhs_s[h], preferred_element_type=F32)
        u_s[h] = uw[:, 0:hd]
        w_s[h] = uw[:, hd:2 * hd].astype(BF16)

    gain = gain_ref[...]
    zero_v = jnp.zeros((c, hd), BF16)
    for ci in range(GDN_PACK):
        r0 = ci * c
        slab = (r0 // hd) * hd
        wss = [_mm(jnp.concatenate([w_s[h, r0:r0 + c, :], qd_s[h, r0:r0 + c, :]], axis=0), state[h]) for h in hs]
        for h in hs:
            v_new = u_s[h, r0:r0 + c, :] - wss[h][0:c]
            v_new_b = v_new.astype(BF16)
            pieces = [zero_v] * (hd // c)
            pieces[(r0 - slab) // c] = v_new_b
            out = wss[h][c:2 * c] + jnp.dot(qk_s[h, r0:r0 + c, slab:slab + hd], jnp.concatenate(pieces, axis=0),
                                            preferred_element_type=F32)
            state[h] = (state[h] * jnp.exp(glast[r0:r0 + 1, h:h + 1])
                        + _mm_tn(kd_s[h, r0:r0 + c, :], v_new_b))
            o = out * lax.rsqrt(jnp.mean(out * out, axis=-1, keepdims=True) + RMS_EPS) * gain
            zz = z_ref[r0:r0 + c, h * hd:(h + 1) * hd].astype(F32)
            o_ref[r0:r0 + c, h * hd:(h + 1) * hd] = (o * _silu(zz)).astype(o_ref.dtype)


def _gdn(proj, ab, alog_row, dtb_row, gain_row, layer, batch, seq, heads, cols):
    width = heads * LANES
    ts = GDN_PACK * GDN_CHUNK
    nt = seq // ts
    kern = functools.partial(_gdn_kernel, heads=heads)
    return pl.pallas_call(
        kern,
        grid=(batch, nt),
        in_specs=[
            pl.BlockSpec((ts, width), lambda b, t: (b * nt + t, cols[0])),
            pl.BlockSpec((ts, width), lambda b, t: (b * nt + t, cols[1])),
            pl.BlockSpec((ts, width), lambda b, t: (b * nt + t, cols[2])),
            pl.BlockSpec((ts, width), lambda b, t: (b * nt + t, cols[3])),
            pl.BlockSpec((ts, LANES), lambda b, t: (b * nt + t, 0)),
            pl.BlockSpec((None, 1, LANES), lambda b, t: (layer, 0, 0)),
            pl.BlockSpec((None, 1, LANES), lambda b, t: (layer, 0, 0)),
            pl.BlockSpec((None, 1, LANES), lambda b, t: (layer, 0, 0)),
        ],
        out_specs=pl.BlockSpec((ts, width), lambda b, t: (b * nt + t, 0)),
        out_shape=jax.ShapeDtypeStruct((batch * seq, width), BF16),
        scratch_shapes=[
            pltpu.VMEM((heads, LANES, LANES), F32),
            pltpu.VMEM((heads, ts, ts), BF16),
            pltpu.VMEM((heads, ts, 2 * LANES), BF16),
            pltpu.VMEM((heads, ts, LANES), F32),
            pltpu.VMEM((heads, ts, LANES), BF16),
            pltpu.VMEM((heads, ts, LANES), BF16),
            pltpu.VMEM((heads, ts, LANES), BF16),
        ],
        compiler_params=pltpu.CompilerParams(
            dimension_semantics=("arbitrary", "arbitrary"), vmem_limit_bytes=VMEM_LIMIT),
        name="gdn",
    )(proj, proj, proj, proj, ab, alog_row, dtb_row, gain_row)


def _lru_kernel(xc_ref, yb_ref, wa_ref, ba_ref, wx_ref, bx_ref, lam_ref, o_ref,
                abuf, bbuf, hcar, *, blocks, ts):
    bd = LANES
    width = blocks * bd

    @pl.when(pl.program_id(1) == 0)
    def _():
        hcar[...] = jnp.zeros_like(hcar)

    for n in range(blocks):
        lo = n * bd
        xcb = xc_ref[:, lo:lo + bd]
        xc = xcb.astype(F32)
        r = jax.nn.sigmoid(jnp.dot(xcb, wa_ref[n], preferred_element_type=F32) + ba_ref[0:1, lo:lo + bd])
        i = jax.nn.sigmoid(jnp.dot(xcb, wx_ref[n], preferred_element_type=F32) + bx_ref[0:1, lo:lo + bd])
        log_a = (-LRU_C) * r * _softplus(-lam_ref[0:1, lo:lo + bd])
        a = jnp.exp(log_a)
        one_minus_a2 = -jnp.tanh(log_a) * (1.0 + a * a)
        abuf[:, lo:lo + bd] = a
        bbuf[:, lo:lo + bd] = jnp.sqrt(one_minus_a2) * (i * xc)

    rowi = lax.broadcasted_iota(jnp.int32, (SUBLANES, width), 0)

    def group(gi, hprev):
        r0 = pl.multiple_of(gi * SUBLANES, SUBLANES)
        a = abuf[pl.ds(r0, SUBLANES), :]
        b = bbuf[pl.ds(r0, SUBLANES), :]
        s = 1
        while s < SUBLANES:
            keep = rowi >= s
            a_sh = jnp.where(keep, pltpu.roll(a, s, axis=0), 1.0)
            b_sh = jnp.where(keep, pltpu.roll(b, s, axis=0), 0.0)
            b = a * b_sh + b
            a = a * a_sh
            s *= 2
        bbuf[pl.ds(r0, SUBLANES), :] = a * hprev + b
        return a[SUBLANES - 1:SUBLANES, :] * hprev + b[SUBLANES - 1:SUBLANES, :]

    hcar[...] = lax.fori_loop(0, ts // SUBLANES, group, hcar[...], unroll=4)
    o_ref[...] = (bbuf[...] * _gelu_tanh(yb_ref[...].astype(F32))).astype(o_ref.dtype)


def _lru(proj, w_a, b_a, w_x, b_x, lam, layer, batch, seq, blocks, xb_col, yb_col, ts):
    width = blocks * LANES
    nt = seq // ts
    kern = functools.partial(_lru_kernel, blocks=blocks, ts=ts)
    row = lambda b, t: (layer, 0, 0)
    mat = lambda b, t: (layer, 0, 0, 0)
    return pl.pallas_call(
        kern,
        grid=(batch, nt),
        in_specs=[
            pl.BlockSpec((ts, width), lambda b, t: (b * nt + t, xb_col)),
            pl.BlockSpec((ts, width), lambda b, t: (b * nt + t, yb_col)),
            pl.BlockSpec((None, blocks, LANES, LANES), mat),
            pl.BlockSpec((None, 1, width), row),
            pl.BlockSpec((None, blocks, LANES, LANES), mat),
            pl.BlockSpec((None, 1, width), row),
            pl.BlockSpec((None, 1, width), row),
        ],
        out_specs=pl.BlockSpec((ts, width), lambda b, t: (b * nt + t, 0)),
        out_shape=jax.ShapeDtypeStruct((batch * seq, width), BF16),
        scratch_shapes=[
            pltpu.VMEM((ts, width), F32),
            pltpu.VMEM((ts, width), F32),
            pltpu.VMEM((1, width), F32),
        ],
        compiler_params=pltpu.CompilerParams(
            dimension_semantics=("arbitrary", "arbitrary"), vmem_limit_bytes=VMEM_LIMIT),
        name="lru",
    )(proj, proj, w_a, b_a, w_x, b_x, lam)


def _merge_kernel(x_ref, og_ref, ol_ref, gg0_ref, gg1_ref, gl_ref, wg_ref, wl_ref, wo_ref, c0_ref, c1_ref,
                  o_ref, d0_ref, d1_ref, *, cast_blocks):
    _cast_step(pl.program_id(0), (c0_ref, c1_ref), (d0_ref, d1_ref), cast_blocks)
    pg = jnp.dot(og_ref[...], wg_ref[...], preferred_element_type=F32)
    pb = jnp.dot(ol_ref[...], wl_ref[...], preferred_element_type=F32)
    gg = jnp.concatenate([gg0_ref[...], gg1_ref[...]], axis=1)
    merged = (jax.nn.sigmoid(gg.astype(F32)) * pg
              + jax.nn.sigmoid(gl_ref[...].astype(F32)) * pb)
    o_ref[...] = x_ref[...] + jnp.dot(merged.astype(BF16), wo_ref[...], preferred_element_type=F32)


def _merge(x2, o_gdn, o_lru, proj, w_bg, w_bl, w_out, casts, layer, gg_cols, gl_col, tm):
    m, d = x2.shape
    wg = o_gdn.shape[1]
    wl = o_lru.shape[1]
    const = lambda i: (0, 0)
    steps = m // tm
    cast = [_cast_specs(w, layer, steps, lambda i: i) for w in casts]
    kern = functools.partial(_merge_kernel, cast_blocks=tuple(_cast_plan(w, steps)[0] for w in casts))
    return pl.pallas_call(
        kern,
        grid=(steps,),
        in_specs=[
            pl.BlockSpec((tm, d), lambda i: (i, 0)),
            pl.BlockSpec((tm, wg), lambda i: (i, 0)),
            pl.BlockSpec((tm, wl), lambda i: (i, 0)),
            pl.BlockSpec((tm, d // 2), lambda i: (i, gg_cols[0])),
            pl.BlockSpec((tm, d // 2), lambda i: (i, gg_cols[1])),
            pl.BlockSpec((tm, d), lambda i: (i, gl_col)),
            pl.BlockSpec((wg, d), const, pipeline_mode=pl.Buffered(1)),
            pl.BlockSpec((wl, d), const, pipeline_mode=pl.Buffered(1)),
            pl.BlockSpec((d, d), const, pipeline_mode=pl.Buffered(1)),
        ] + [c[0] for c in cast],
        out_specs=[pl.BlockSpec((tm, d), lambda i: (i, 0))] + [c[1] for c in cast],
        out_shape=[jax.ShapeDtypeStruct((m, d), F32)] + [c[2] for c in cast],
        compiler_params=pltpu.CompilerParams(
            dimension_semantics=("arbitrary",), vmem_limit_bytes=VMEM_LIMIT),
        name="merge",
    )(x2, o_gdn, o_lru, proj, proj, proj, w_bg, w_bl, w_out, *casts)


def _mlp_kernel(x_ref, g_ref, wu_ref, wd_ref, fg_ref, o_ref, h_ref, *, final_norm):
    f = pl.program_id(1)

    @pl.when(f == 0)
    def _():
        x = x_ref[...]
        ms = jnp.mean(x * x, axis=-1, keepdims=True)
        h_ref[...] = (x * lax.rsqrt(ms + RMS_EPS) * g_ref[...]).astype(BF16)
        o_ref[...] = x

    u = jnp.dot(h_ref[...], wu_ref[...], preferred_element_type=F32)
    u = jnp.maximum(u, 0.0)
    o_ref[...] += jnp.dot((u * u).astype(BF16), wd_ref[...], preferred_element_type=F32)

    if final_norm:
        @pl.when(f == pl.num_programs(1) - 1)
        def _():
            y = o_ref[...]
            ms = jnp.mean(y * y, axis=-1, keepdims=True)
            o_ref[...] = y * lax.rsqrt(ms + RMS_EPS) * fg_ref[...]


def _mlp(x2, gain, w_up, w_down, final_gain, layer, final_norm, tm, tf):
    m, d = x2.shape
    ff = w_up.shape[1]
    kern = functools.partial(_mlp_kernel, final_norm=final_norm)
    return pl.pallas_call(
        kern,
        grid=(m // tm, ff // tf),
        in_specs=[
            pl.BlockSpec((tm, d), lambda i, f: (i, 0)),
            pl.BlockSpec((None, 1, d), lambda i, f: (layer, 0, 0)),
            pl.BlockSpec((d, tf), lambda i, f: (0, f)),
            pl.BlockSpec((tf, d), lambda i, f: (f, 0)),
            pl.BlockSpec((1, d), lambda i, f: (0, 0)),
        ],
        out_specs=pl.BlockSpec((tm, d), lambda i, f: (i, 0)),
        out_shape=jax.ShapeDtypeStruct((m, d), F32),
        scratch_shapes=[pltpu.VMEM((tm, d), BF16)],
        compiler_params=pltpu.CompilerParams(
            dimension_semantics=("arbitrary", "arbitrary"), vmem_limit_bytes=VMEM_LIMIT),
        name="mlp",
    )(x2, gain, w_up, w_down, final_gain)


def _tile(n, want):
    t = min(want, n)
    assert n % t == 0, (n, want)
    return t


def kernel(x, attn_norm, w_in, gdn_conv_w, gdn_a_log, gdn_dt_bias, gdn_norm, lru_conv_w, lru_conv_b,
           lru_w_a, lru_b_a, lru_w_x, lru_b_x, lru_lambda, w_branch_gdn, w_branch_lru, w_out,
           mlp_norm, w_up, w_down, final_norm):
    batch, seq, d = x.shape
    depth = w_in.shape[0]
    heads = gdn_a_log.shape[1]
    gw = heads * LANES
    assert gdn_norm.shape[1] == LANES and gdn_conv_w.shape[2] == 3 * gw
    blocks = lru_w_a.shape[1]
    lw = blocks * LANES
    assert lru_w_a.shape[2] == LANES
    assert 2 * heads <= LANES and seq % (GDN_PACK * GDN_CHUNK) == 0
    assert gw == lw
    m = batch * seq

    o_z = 3 * gw
    o_a = o_z + gw
    o_xb = o_a + 2 * heads
    o_yb = o_xb + lw
    o_gg = o_yb + lw
    o_gl = o_gg + d
    assert w_in.shape[2] == o_gl + d
    assert d == 2 * gw
    pieces = [(0, gw), (o_z, gw), (gw, gw), (o_yb, lw), (2 * gw, gw), (o_gg, gw), (o_xb, lw), (o_gg + gw, gw),
              (o_gl, d)]
    gdn_cols = (0, 2, 4, 1)
    yb_col, xb_col = 3, 6
    gg_cols, gl_col = (5, 7), 4
    lru_step = 3

    tm_in = _tile(seq, 1024)
    tn_in = 2 * gw
    ts_lru = _tile(seq, 512)
    tm_merge = _tile(m, 256)
    tm_mlp = _tile(m, 1024)
    tf_mlp = _tile(w_up.shape[2], 512)

    w_main = jnp.concatenate([w_in[:, :, o:o + n] for o, n in pieces], axis=2).astype(BF16)
    w_ab = w_in[:, :, o_a:o_a + LANES].astype(BF16)
    zeros_w = jnp.zeros((depth, CONV_WIDTH, gw), F32)
    conv_w = jnp.concatenate([gdn_conv_w.astype(F32), lru_conv_w.astype(F32), zeros_w], axis=2)
    conv_b = jnp.pad(lru_conv_b.astype(F32), ((0, 0), (3 * gw, gw)))[:, None, :]
    pad_h = ((0, 0), (0, LANES - heads))
    alog_rows = jnp.pad(gdn_a_log.astype(F32), pad_h)[:, None, :]
    dtb_rows = jnp.pad(gdn_dt_bias.astype(F32), pad_h)[:, None, :]
    w_a_b, w_x_b = lru_w_a.astype(BF16), lru_w_x.astype(BF16)
    row3 = lambda v: v[:, None, :]

    x2 = x.reshape(m, d)
    for l in range(depth):
        proj, ab, w_bg_b, w_bl_b, w_out_b = _in_proj(
            x2, row3(attn_norm), w_main, w_ab, conv_w, conv_b, (w_branch_gdn, w_branch_lru, w_out),
            l, tm_in, tn_in, seq, lru_step)
        o_gdn = _gdn(proj, ab, alog_rows, dtb_rows, row3(gdn_norm), l, batch, seq, heads, gdn_cols)
        o_lru = _lru(proj, w_a_b, row3(lru_b_a), w_x_b, row3(lru_b_x),
                     row3(lru_lambda), l, batch, seq, blocks, xb_col, yb_col, ts_lru)
        x2, w_up_b, w_down_b = _merge(x2, o_gdn, o_lru, proj, w_bg_b, w_bl_b, w_out_b, (w_up, w_down),
                                      l, gg_cols, gl_col, tm_merge)
        x2 = _mlp(x2, row3(mlp_norm), w_up_b, w_down_b, final_norm.reshape(1, d), l, l == depth - 1,
                  tm_mlp, tf_mlp)
    if depth == 0:
        raise ValueError("depth must be positive")
    return x2.reshape(batch, seq, d)
```

```python
import functools

import jax
import jax.numpy as jnp
from jax import lax
from jax.experimental import pallas as pl
from jax.experimental.pallas import tpu as pltpu

F32 = jnp.float32
BF16 = jnp.bfloat16

RMS_EPS = 1e-6
L2_EPS = 1e-6
LRU_C = 8.0
CONV_WIDTH = 4
GDN_CHUNK = 64
INV_BLOCK = 8
GDN_PACK = 4
CONV_COLS = 256
LANES = 128
SUBLANES = 8
BF16_ROWS = 16
VMEM_LIMIT = 56 * 1024 * 1024


def _mm(a, b):
    return jnp.dot(a.astype(BF16), b.astype(BF16), preferred_element_type=F32)


def _mm_tn(a, b):
    return lax.dot_general(a.astype(BF16), b.astype(BF16), (((0,), (0,)), ((), ())),
                           preferred_element_type=F32)


def _softplus(x):
    return jnp.maximum(x, 0.0) + jnp.log1p(jnp.exp(-jnp.abs(x)))


def _silu(x):
    h = 0.5 * x
    return h + h * jnp.tanh(h)


def _gelu_tanh(x):
    c = 0.7978845608028654
    return 0.5 * x * (1.0 + jnp.tanh(c * (x + 0.044715 * (x * x * x))))


def _cast_plan(w, steps):
    rows = w.shape[1]
    units = rows // BF16_ROWS
    assert rows % BF16_ROWS == 0
    nb = max(k for k in range(1, min(units, steps) + 1) if units % k == 0)
    return nb, rows // nb


def _cast_specs(w, layer, steps, step_of):
    nb, rb = _cast_plan(w, steps)
    cols = w.shape[2]
    blk = lambda *ids: jnp.minimum(step_of(*ids), nb - 1)
    return (pl.BlockSpec((None, rb, cols), lambda *ids: (layer, blk(*ids), 0)),
            pl.BlockSpec((rb, cols), lambda *ids: (blk(*ids), 0)),
            jax.ShapeDtypeStruct((w.shape[1], cols), BF16))


def _cast_step(step, srcs, dsts, nbs):
    for src, dst, nb in zip(srcs, dsts, nbs):
        @pl.when(step < nb)
        def _(src=src, dst=dst):
            dst[...] = src[...].astype(dst.dtype)


def _regroup_kernel(w_ref, o_ref, oab_ref, *, pieces, ab_start):
    col = 0
    for start, width in pieces:
        o_ref[:, col:col + width] = w_ref[:, start:start + width].astype(o_ref.dtype)
        col += width
    oab_ref[...] = w_ref[:, ab_start:ab_start + LANES].astype(oab_ref.dtype)


def _regroup_w_in(w_in, pieces, ab_start, rows):
    depth, d, n_in = w_in.shape
    n_out = sum(width for _, width in pieces)
    kern = functools.partial(_regroup_kernel, pieces=tuple(pieces), ab_start=ab_start)
    return pl.pallas_call(
        kern,
        grid=(depth, d // rows),
        in_specs=[pl.BlockSpec((None, rows, n_in), lambda l, r: (l, r, 0))],
        out_specs=[pl.BlockSpec((None, rows, n_out), lambda l, r: (l, r, 0)),
                   pl.BlockSpec((None, rows, LANES), lambda l, r: (l, r, 0))],
        out_shape=[jax.ShapeDtypeStruct((depth, d, n_out), BF16),
                   jax.ShapeDtypeStruct((depth, d, LANES), BF16)],
        compiler_params=pltpu.CompilerParams(
            dimension_semantics=("arbitrary", "arbitrary"), vmem_limit_bytes=VMEM_LIMIT),
        name="regroup_w_in",
    )(w_in)


def _in_proj_kernel(x_ref, g_ref, w_ref, wab_ref, cw_ref, cb_ref, c0_ref, c1_ref, c2_ref,
                    o_ref, oab_ref, d0_ref, d1_ref, d2_ref, h_ref, tail_ref, *,
                    tiles_per_seq, lru_step, cast_blocks):
    i = pl.program_id(0)
    j = pl.program_id(1)
    tm, tn = o_ref.shape
    half = tn // 2
    _cast_step(i * pl.num_programs(1) + j, (c0_ref, c1_ref, c2_ref), (d0_ref, d1_ref, d2_ref), cast_blocks)

    @pl.when(j == 0)
    def _():
        x = x_ref[...]
        ms = jnp.mean(x * x, axis=-1, keepdims=True)
        h = (x * lax.rsqrt(ms + RMS_EPS) * g_ref[...]).astype(BF16)
        h_ref[...] = h
        oab_ref[...] = jnp.dot(h, wab_ref[...], preferred_element_type=F32)

        @pl.when(i == 0)
        def _():
            tail_ref[...] = jnp.zeros_like(tail_ref)

    cc = min(half, CONV_COLS)
    row8 = lax.broadcasted_iota(jnp.int32, (SUBLANES, cc), 0)

    def shift_rows(val, prev8, s):
        rolled = pltpu.roll(val, s, axis=0)
        top = jnp.where(row8 < s, pltpu.roll(prev8, s, axis=0), rolled[0:SUBLANES, :])
        return jnp.concatenate([top, rolled[SUBLANES:, :]], axis=0)

    def conv_chunks(finish):
        first = i % tiles_per_seq == 0
        nb = half // cc
        for ci in range(nb):
            c0 = ci * cc
            col = pl.multiple_of(j * half, half) + c0
            prev = jnp.where(first, 0.0, tail_ref[:, pl.ds(col, cc)])
            res = jnp.dot(h_ref[...], w_ref[:, c0:c0 + cc], preferred_element_type=F32)
            tail_ref[:, pl.ds(col, cc)] = res[tm - SUBLANES:tm, :]
            b0 = half + c0
            o_ref[:, b0:b0 + cc] = jnp.dot(h_ref[...], w_ref[:, b0:b0 + cc],
                                           preferred_element_type=F32).astype(o_ref.dtype)
            taps = [cw_ref[k:k + 1, c0:c0 + cc] for k in range(CONV_WIDTH)]
            s1 = shift_rows(res, prev, 1)
            u = res * taps[3] + s1 * taps[2]
            v = res * taps[1] + s1 * taps[0]
            prev_v = prev * taps[1] + pltpu.roll(prev, 1, axis=0) * taps[0]
            finish(u + shift_rows(v, prev_v, 2), c0)

    @pl.when(j < 3)
    def _():
        is_v = j == 2
        q_scale = jnp.where(j == 0, LANES ** -0.5, 1.0)

        def finish(acc, c0):
            y = _silu(acc)
            for g in range(cc // LANES):
                yg = y[:, g * LANES:(g + 1) * LANES]
                inv = lax.rsqrt(jnp.sum(yg * yg, axis=-1, keepdims=True) + L2_EPS)
                o_ref[:, c0 + g * LANES:c0 + (g + 1) * LANES] = (
                    yg * jnp.where(is_v, 1.0, inv * q_scale)).astype(o_ref.dtype)

        conv_chunks(finish)

    @pl.when(j == lru_step)
    def _():
        def finish(acc, c0):
            o_ref[:, c0:c0 + cc] = (acc + cb_ref[:, c0:c0 + cc]).astype(o_ref.dtype)

        conv_chunks(finish)

    @pl.when((j >= 3) & (j != lru_step))
    def _():
        o_ref[...] = jnp.dot(h_ref[...], w_ref[...], preferred_element_type=F32).astype(o_ref.dtype)


def _in_proj(x2, gain, w_main, w_ab, conv_w, conv_b, casts, layer, tm, tn, seq, lru_step):
    m, d = x2.shape
    n = w_main.shape[2]
    half = tn // 2
    nj = n // tn
    steps = (m // tm) * nj
    cast = [_cast_specs(w, layer, steps, lambda i, j: i * nj + j) for w in casts]
    kern = functools.partial(_in_proj_kernel, tiles_per_seq=seq // tm, lru_step=lru_step,
                             cast_blocks=tuple(_cast_plan(w, steps)[0] for w in casts))
    return pl.pallas_call(
        kern,
        grid=(m // tm, nj),
        in_specs=[
            pl.BlockSpec((tm, d), lambda i, j: (i, 0)),
            pl.BlockSpec((None, 1, d), lambda i, j: (layer, 0, 0)),
            pl.BlockSpec((None, d, tn), lambda i, j: (layer, 0, j)),
            pl.BlockSpec((None, d, LANES), lambda i, j: (layer, 0, 0)),
            pl.BlockSpec((None, CONV_WIDTH, half), lambda i, j: (layer, 0, j)),
            pl.BlockSpec((None, 1, half), lambda i, j: (layer, 0, j)),
        ] + [c[0] for c in cast],
        out_specs=[
            pl.BlockSpec((tm, tn), lambda i, j: (i, j)),
            pl.BlockSpec((tm, LANES), lambda i, j: (i, 0)),
        ] + [c[1] for c in cast],
        out_shape=[
            jax.ShapeDtypeStruct((m, n), BF16),
            jax.ShapeDtypeStruct((m, LANES), F32),
        ] + [c[2] for c in cast],
        scratch_shapes=[pltpu.VMEM((tm, d), BF16), pltpu.VMEM((SUBLANES, n // 2), F32)],
        compiler_params=pltpu.CompilerParams(
            dimension_semantics=("arbitrary", "arbitrary"), vmem_limit_bytes=VMEM_LIMIT),
        name="in_proj",
    )(x2, gain, w_main, w_ab, conv_w, conv_b, *casts)


def _gdn_kernel(q_ref, k_ref, v_ref, z_ref, ab_ref, alog_ref, dtb_ref, gain_ref, o_ref,
                state, qk_s, rhs_s, u_s, w_s, qd_s, kd_s, *, heads):
    hd = LANES
    width = heads * hd
    c = GDN_CHUNK
    ts = GDN_PACK * c
    hs = range(heads)

    @pl.when(pl.program_id(1) == 0)
    def _():
        state[...] = jnp.zeros_like(state)

    ab = ab_ref[...]
    g_all = -jnp.exp(alog_ref[...]) * _softplus(ab + dtb_ref[...])
    beta_all = jax.nn.sigmoid(ab)

    r2 = lax.broadcasted_iota(jnp.int32, (ts, ts), 0)
    c2 = lax.broadcasted_iota(jnp.int32, (ts, ts), 1)
    same_chunk = (r2 & -c) == (c2 & -c)
    causal_bd = same_chunk & (r2 >= c2)
    strict_bd = same_chunk & (r2 > c2)
    rp = lax.broadcasted_iota(jnp.int32, (c, ts), 0)
    cp = lax.broadcasted_iota(jnp.int32, (c, ts), 1) & (c - 1)
    eye_p = jnp.where(rp == cp, 1.0, 0.0)
    in_diag_block = (rp & -INV_BLOCK) == (cp & -INV_BLOCK)

    lane = lax.broadcasted_iota(jnp.int32, (c, hd), 1)
    lane_masks = [jnp.where((lane & -c) == p * c, 1.0, 0.0).astype(BF16) for p in range(hd // c)]
    zero_blk = jnp.zeros((c, hd), BF16)

    def blockdiag(y):
        yb = y.astype(BF16)
        rows = []
        for i in range(GDN_PACK):
            slab, off = divmod(i * c, hd)
            blocks = [zero_blk] * (ts // hd)
            blocks[slab] = yb[:, slab * hd:(slab + 1) * hd] * lane_masks[off // c]
            rows.append(jnp.concatenate(blocks, axis=1))
        return jnp.concatenate(rows, axis=0)

    def bmm(x, bd):
        return jnp.dot(x.astype(BF16), bd, preferred_element_type=F32)

    g_hi = g_all.astype(BF16)
    r_mid = g_all - g_hi.astype(F32)
    g_mid = r_mid.astype(BF16)
    g_lo = (r_mid - g_mid.astype(F32)).astype(BF16)
    sums = jnp.dot(jnp.where(causal_bd, 1.0, 0.0).astype(BF16), jnp.concatenate([g_hi, g_mid, g_lo], axis=1),
                   preferred_element_type=F32)
    gcum = sums[:, 0:LANES] + sums[:, LANES:2 * LANES] + sums[:, 2 * LANES:3 * LANES]
    glast = jnp.concatenate([jnp.broadcast_to(gcum[(ci + 1) * c - 1:(ci + 1) * c, :], (c, LANES))
                             for ci in range(GDN_PACK)], axis=0)
    gcum_t = gcum.T

    prods = []
    for h in hs:
        qb = q_ref[:, h * hd:(h + 1) * hd]
        kb16 = k_ref[:, h * hd:(h + 1) * hd]
        q = qb.astype(F32)
        k = kb16.astype(F32)
        v = v_ref[:, h * hd:(h + 1) * hd].astype(F32)
        gcol = gcum[:, h:h + 1]
        egc = jnp.exp(gcol)
        beta = beta_all[:, heads + h:heads + h + 1]
        kb = k * beta
        prods.append(lax.dot_general(jnp.concatenate([qb, kb.astype(BF16)], axis=0), kb16,
                                     (((1,), (1,)), ((), ())), preferred_element_type=F32))
        rhs_s[h] = jnp.concatenate([v * beta, kb * egc], axis=1).astype(BF16)
        qd_s[h] = (q * egc).astype(BF16)
        kd_s[h] = (k * jnp.exp(glast[:, h:h + 1] - gcol)).astype(BF16)
    lps = []
    for h in hs:
        gcol = gcum[:, h:h + 1]
        grow = gcum_t[h:h + 1, :]
        dmat = jnp.where(causal_bd, jnp.exp(jnp.where(causal_bd, gcol - grow, 0.0)), 0.0)
        qk_s[h] = (prods[h][0:ts] * dmat).astype(BF16)
        lbd = jnp.where(strict_bd, prods[h][ts:2 * ts] * dmat, 0.0)
        lp = lbd[0:c]
        for i in range(1, GDN_PACK):
            lp = lp + lbd[i * c:(i + 1) * c]
        lps.append(lp)

    ld = [jnp.where(in_diag_block, lp, 0.0) for lp in lps]
    tinv = [eye_p - d for d in ld]
    pw = [bmm(d, blockdiag(d)) for d in ld]
    span = 2
    while span < INV_BLOCK:
        last = 2 * span >= INV_BLOCK
        res = [bmm(d if last else jnp.concatenate([p, d], axis=0), blockdiag(p)) for p, d in zip(pw, tinv)]
        if last:
            tinv = [d + r for d, r in zip(tinv, res)]
        else:
            pw = [r[0:c] for r in res]
            tinv = [d + r[c:2 * c] for d, r in zip(tinv, res)]
        span *= 2
    b = INV_BLOCK
    while b < c:
        off = ((rp & -2 * b) == (cp & -2 * b)) & ((rp & -b) != (cp & -b))
        te = [bmm(x, blockdiag(jnp.where(off, lp, 0.0))) for x, lp in zip(tinv, lps)]
        tinv = [x - bmm(y, blockdiag(x)) for x, y in zip(tinv, te)]
        b *= 2

    for h in hs:
        uw = jnp.dot(blockdiag(tinv[h]), rhs_s[h], preferred_element_type=F32)
        u_s[h] = uw[:, 0:hd]
        w_s[h] = uw[:, hd:2 * hd].astype(BF16)

    gain = gain_ref[...]
    zero_v = jnp.zeros((c, hd), BF16)
    for ci in range(GDN_PACK):
        r0 = ci * c
        slab = (r0 // hd) * hd
        wss = [_mm(jnp.concatenate([w_s[h, r0:r0 + c, :], qd_s[h, r0:r0 + c, :]], axis=0), state[h]) for h in hs]
        for h in hs:
            v_new = u_s[h, r0:r0 + c, :] - wss[h][0:c]
            v_new_b = v_new.astype(BF16)
            pieces = [zero_v] * (hd // c)
            pieces[(r0 - slab) // c] = v_new_b
            out = wss[h][c:2 * c] + jnp.dot(qk_s[h, r0:r0 + c, slab:slab + hd], jnp.concatenate(pieces, axis=0),
                                            preferred_element_type=F32)
            state[h] = (state[h] * jnp.exp(glast[r0:r0 + 1, h:h + 1])
                        + _mm_tn(kd_s[h, r0:r0 + c, :], v_new_b))
            o = out * lax.rsqrt(jnp.mean(out * out, axis=-1, keepdims=True) + RMS_EPS) * gain
            zz = z_ref[r0:r0 + c, h * hd:(h + 1) * hd].astype(F32)
            o_ref[r0:r0 + c, h * hd:(h + 1) * hd] = (o * _silu(zz)).astype(o_ref.dtype)


def _gdn(proj, ab, alog_row, dtb_row, gain_row, layer, batch, seq, heads, cols):
    width = heads * LANES
    ts = GDN_PACK * GDN_CHUNK
    nt = seq // ts
    kern = functools.partial(_gdn_kernel, heads=heads)
    return pl.pallas_call(
        kern,
        grid=(batch, nt),
        in_specs=[
            pl.BlockSpec((ts, width), lambda b, t: (b * nt + t, cols[0])),
            pl.BlockSpec((ts, width), lambda b, t: (b * nt + t, cols[1])),
            pl.BlockSpec((ts, width), lambda b, t: (b * nt + t, cols[2])),
            pl.BlockSpec((ts, width), lambda b, t: (b * nt + t, cols[3])),
            pl.BlockSpec((ts, LANES), lambda b, t: (b * nt + t, 0)),
            pl.BlockSpec((None, 1, LANES), lambda b, t: (layer, 0, 0)),
            pl.BlockSpec((None, 1, LANES), lambda b, t: (layer, 0, 0)),
            pl.BlockSpec((None, 1, LANES), lambda b, t: (layer, 0, 0)),
        ],
        out_specs=pl.BlockSpec((ts, width), lambda b, t: (b * nt + t, 0)),
        out_shape=jax.ShapeDtypeStruct((batch * seq, width), BF16),
        scratch_shapes=[
            pltpu.VMEM((heads, LANES, LANES), F32),
            pltpu.VMEM((heads, ts, ts), BF16),
            pltpu.VMEM((heads, ts, 2 * LANES), BF16),
            pltpu.VMEM((heads, ts, LANES), F32),
            pltpu.VMEM((heads, ts, LANES), BF16),
            pltpu.VMEM((heads, ts, LANES), BF16),
            pltpu.VMEM((heads, ts, LANES), BF16),
        ],
        compiler_params=pltpu.CompilerParams(
            dimension_semantics=("arbitrary", "arbitrary"), vmem_limit_bytes=VMEM_LIMIT),
        name="gdn",
    )(proj, proj, proj, proj, ab, alog_row, dtb_row, gain_row)


def _lru_kernel(xc_ref, yb_ref, wa_ref, ba_ref, wx_ref, bx_ref, lam_ref, o_ref,
                abuf, bbuf, hcar, *, blocks, ts):
    bd = LANES
    width = blocks * bd

    @pl.when(pl.program_id(1) == 0)
    def _():
        hcar[...] = jnp.zeros_like(hcar)

    for n in range(blocks):
        lo = n * bd
        xcb = xc_ref[:, lo:lo + bd]
        xc = xcb.astype(F32)
        r = jax.nn.sigmoid(jnp.dot(xcb, wa_ref[n], preferred_element_type=F32) + ba_ref[0:1, lo:lo + bd])
        i = jax.nn.sigmoid(jnp.dot(xcb, wx_ref[n], preferred_element_type=F32) + bx_ref[0:1, lo:lo + bd])
        log_a = (-LRU_C) * r * _softplus(-lam_ref[0:1, lo:lo + bd])
        a = jnp.exp(log_a)
        one_minus_a2 = -jnp.tanh(log_a) * (1.0 + a * a)
        abuf[:, lo:lo + bd] = a
        bbuf[:, lo:lo + bd] = jnp.sqrt(one_minus_a2) * (i * xc)

    rowi = lax.broadcasted_iota(jnp.int32, (SUBLANES, width), 0)

    def group(gi, hprev):
        r0 = pl.multiple_of(gi * SUBLANES, SUBLANES)
        a = abuf[pl.ds(r0, SUBLANES), :]
        b = bbuf[pl.ds(r0, SUBLANES), :]
        s = 1
        while s < SUBLANES:
            keep = rowi >= s
            a_sh = jnp.where(keep, pltpu.roll(a, s, axis=0), 1.0)
            b_sh = jnp.where(keep, pltpu.roll(b, s, axis=0), 0.0)
            b = a * b_sh + b
            a = a * a_sh
            s *= 2
        bbuf[pl.ds(r0, SUBLANES), :] = a * hprev + b
        return a[SUBLANES - 1:SUBLANES, :] * hprev + b[SUBLANES - 1:SUBLANES, :]

    hcar[...] = lax.fori_loop(0, ts // SUBLANES, group, hcar[...], unroll=4)
    o_ref[...] = (bbuf[...] * _gelu_tanh(yb_ref[...].astype(F32))).astype(o_ref.dtype)


def _lru(proj, w_a, b_a, w_x, b_x, lam, layer, batch, seq, blocks, xb_col, yb_col, ts):
    width = blocks * LANES
    nt = seq // ts
    kern = functools.partial(_lru_kernel, blocks=blocks, ts=ts)
    row = lambda b, t: (layer, 0, 0)
    mat = lambda b, t: (layer, 0, 0, 0)
    return pl.pallas_call(
        kern,
        grid=(batch, nt),
        in_specs=[
            pl.BlockSpec((ts, width), lambda b, t: (b * nt + t, xb_col)),
            pl.BlockSpec((ts, width), lambda b, t: (b * nt + t, yb_col)),
            pl.BlockSpec((None, blocks, LANES, LANES), mat),
            pl.BlockSpec((None, 1, width), row),
            pl.BlockSpec((None, blocks, LANES, LANES), mat),
            pl.BlockSpec((None, 1, width), row),
            pl.BlockSpec((None, 1, width), row),
        ],
        out_specs=pl.BlockSpec((ts, width), lambda b, t: (b * nt + t, 0)),
        out_shape=jax.ShapeDtypeStruct((batch * seq, width), BF16),
        scratch_shapes=[
            pltpu.VMEM((ts, width), F32),
            pltpu.VMEM((ts, width), F32),
            pltpu.VMEM((1, width), F32),
        ],
        compiler_params=pltpu.CompilerParams(
            dimension_semantics=("arbitrary", "arbitrary"), vmem_limit_bytes=VMEM_LIMIT),
        name="lru",
    )(proj, proj, w_a, b_a, w_x, b_x, lam)


def _merge_kernel(x_ref, og_ref, ol_ref, gg0_ref, gg1_ref, gl_ref, wg_ref, wl_ref, wo_ref, c0_ref, c1_ref,
                  o_ref, d0_ref, d1_ref, *, cast_blocks):
    _cast_step(pl.program_id(0), (c0_ref, c1_ref), (d0_ref, d1_ref), cast_blocks)
    pg = jnp.dot(og_ref[...], wg_ref[...], preferred_element_type=F32)
    pb = jnp.dot(ol_ref[...], wl_ref[...], preferred_element_type=F32)
    gg = jnp.concatenate([gg0_ref[...], gg1_ref[...]], axis=1)
    merged = (jax.nn.sigmoid(gg.astype(F32)) * pg
              + jax.nn.sigmoid(gl_ref[...].astype(F32)) * pb)
    o_ref[...] = x_ref[...] + jnp.dot(merged.astype(BF16), wo_ref[...], preferred_element_type=F32)


def _merge(x2, o_gdn, o_lru, proj, w_bg, w_bl, w_out, casts, layer, gg_cols, gl_col, tm):
    m, d = x2.shape
    wg = o_gdn.shape[1]
    wl = o_lru.shape[1]
    const = lambda i: (0, 0)
    steps = m // tm
    cast = [_cast_specs(w, layer, steps, lambda i: i) for w in casts]
    kern = functools.partial(_merge_kernel, cast_blocks=tuple(_cast_plan(w, steps)[0] for w in casts))
    return pl.pallas_call(
        kern,
        grid=(steps,),
        in_specs=[
            pl.BlockSpec((tm, d), lambda i: (i, 0)),
            pl.BlockSpec((tm, wg), lambda i: (i, 0)),
            pl.BlockSpec((tm, wl), lambda i: (i, 0)),
            pl.BlockSpec((tm, d // 2), lambda i: (i, gg_cols[0])),
            pl.BlockSpec((tm, d // 2), lambda i: (i, gg_cols[1])),
            pl.BlockSpec((tm, d), lambda i: (i, gl_col)),
            pl.BlockSpec((wg, d), const, pipeline_mode=pl.Buffered(1)),
            pl.BlockSpec((wl, d), const, pipeline_mode=pl.Buffered(1)),
            pl.BlockSpec((d, d), const, pipeline_mode=pl.Buffered(1)),
        ] + [c[0] for c in cast],
        out_specs=[pl.BlockSpec((tm, d), lambda i: (i, 0))] + [c[1] for c in cast],
        out_shape=[jax.ShapeDtypeStruct((m, d), F32)] + [c[2] for c in cast],
        compiler_params=pltpu.CompilerParams(
            dimension_semantics=("arbitrary",), vmem_limit_bytes=VMEM_LIMIT),
        name="merge",
    )(x2, o_gdn, o_lru, proj, proj, proj, w_bg, w_bl, w_out, *casts)


def _mlp_kernel(x_ref, g_ref, wu_ref, wd_ref, fg_ref, o_ref, h_ref, *, final_norm):
    f = pl.program_id(1)

    @pl.when(f == 0)
    def _():
        x = x_ref[...]
        ms = jnp.mean(x * x, axis=-1, keepdims=True)
        h_ref[...] = (x * lax.rsqrt(ms + RMS_EPS) * g_ref[...]).astype(BF16)
        o_ref[...] = x

    u = jnp.dot(h_ref[...], wu_ref[...], preferred_element_type=F32)
    u = jnp.maximum(u, 0.0)
    o_ref[...] += jnp.dot((u * u).astype(BF16), wd_ref[...], preferred_element_type=F32)

    if final_norm:
        @pl.when(f == pl.num_programs(1) - 1)
        def _():
            y = o_ref[...]
            ms = jnp.mean(y * y, axis=-1, keepdims=True)
            o_ref[...] = y * lax.rsqrt(ms + RMS_EPS) * fg_ref[...]


def _mlp(x2, gain, w_up, w_down, final_gain, layer, final_norm, tm, tf):
    m, d = x2.shape
    ff = w_up.shape[1]
    kern = functools.partial(_mlp_kernel, final_norm=final_norm)
    return pl.pallas_call(
        kern,
        grid=(m // tm, ff // tf),
        in_specs=[
            pl.BlockSpec((tm, d), lambda i, f: (i, 0)),
            pl.BlockSpec((None, 1, d), lambda i, f: (layer, 0, 0)),
            pl.BlockSpec((d, tf), lambda i, f: (0, f)),
            pl.BlockSpec((tf, d), lambda i, f: (f, 0)),
            pl.BlockSpec((1, d), lambda i, f: (0, 0)),
        ],
        out_specs=pl.BlockSpec((tm, d), lambda i, f: (i, 0)),
        out_shape=jax.ShapeDtypeStruct((m, d), F32),
        scratch_shapes=[pltpu.VMEM((tm, d), BF16)],
        compiler_params=pltpu.CompilerParams(
            dimension_semantics=("arbitrary", "arbitrary"), vmem_limit_bytes=VMEM_LIMIT),
        name="mlp",
    )(x2, gain, w_up, w_down, final_gain)


def _tile(n, want):
    t = min(want, n)
    assert n % t == 0, (n, want)
    return t


def kernel(x, attn_norm, w_in, gdn_conv_w, gdn_a_log, gdn_dt_bias, gdn_norm, lru_conv_w, lru_conv_b,
           lru_w_a, lru_b_a, lru_w_x, lru_b_x, lru_lambda, w_branch_gdn, w_branch_lru, w_out,
           mlp_norm, w_up, w_down, final_norm):
    batch, seq, d = x.shape
    depth = w_in.shape[0]
    heads = gdn_a_log.shape[1]
    gw = heads * LANES
    assert gdn_norm.shape[1] == LANES and gdn_conv_w.shape[2] == 3 * gw
    blocks = lru_w_a.shape[1]
    lw = blocks * LANES
    assert lru_w_a.shape[2] == LANES
    assert 2 * heads <= LANES and seq % (GDN_PACK * GDN_CHUNK) == 0
    assert gw == lw
    m = batch * seq

    o_z = 3 * gw
    o_a = o_z + gw
    o_xb = o_a + 2 * heads
    o_yb = o_xb + lw
    o_gg = o_yb + lw
    o_gl = o_gg + d
    assert w_in.shape[2] == o_gl + d
    assert d == 2 * gw
    pieces = [(0, gw), (o_z, gw), (gw, gw), (o_yb, lw), (2 * gw, gw), (o_gg, gw), (o_xb, lw), (o_gg + gw, gw),
              (o_gl, d)]
    gdn_cols = (0, 2, 4, 1)
    yb_col, xb_col = 3, 6
    gg_cols, gl_col = (5, 7), 4
    lru_step = 3

    tm_in = _tile(seq, 1024)
    tn_in = 2 * gw
    ts_lru = _tile(seq, 512)
    tm_merge = _tile(m, 256)
    tm_mlp = _tile(m, 1024)
    tf_mlp = _tile(w_up.shape[2], 512)

    w_main, w_ab = _regroup_w_in(w_in, pieces, o_a, _tile(d, 256))
    zeros_w = jnp.zeros((depth, CONV_WIDTH, gw), F32)
    conv_w = jnp.concatenate([gdn_conv_w.astype(F32), lru_conv_w.astype(F32), zeros_w], axis=2)
    conv_b = jnp.pad(lru_conv_b.astype(F32), ((0, 0), (3 * gw, gw)))[:, None, :]
    pad_h = ((0, 0), (0, LANES - heads))
    alog_rows = jnp.pad(gdn_a_log.astype(F32), pad_h)[:, None, :]
    dtb_rows = jnp.pad(gdn_dt_bias.astype(F32), pad_h)[:, None, :]
    w_a_b, w_x_b = lru_w_a.astype(BF16), lru_w_x.astype(BF16)
    row3 = lambda v: v[:, None, :]

    x2 = x.reshape(m, d)
    for l in range(depth):
        proj, ab, w_bg_b, w_bl_b, w_out_b = _in_proj(
            x2, row3(attn_norm), w_main, w_ab, conv_w, conv_b, (w_branch_gdn, w_branch_lru, w_out),
            l, tm_in, tn_in, seq, lru_step)
        o_gdn = _gdn(proj, ab, alog_rows, dtb_rows, row3(gdn_norm), l, batch, seq, heads, gdn_cols)
        o_lru = _lru(proj, w_a_b, row3(lru_b_a), w_x_b, row3(lru_b_x),
                     row3(lru_lambda), l, batch, seq, blocks, xb_col, yb_col, ts_lru)
        x2, w_up_b, w_down_b = _merge(x2, o_gdn, o_lru, proj, w_bg_b, w_bl_b, w_out_b, (w_up, w_down),
                                      l, gg_cols, gl_col, tm_merge)
        x2 = _mlp(x2, row3(mlp_norm), w_up_b, w_down_b, final_norm.reshape(1, d), l, l == depth - 1,
                  tm_mlp, tf_mlp)
    if depth == 0:
        raise ValueError("depth must be positive")
    return x2.reshape(batch, seq, d)
```

```python
import functools

import jax
import jax.numpy as jnp
from jax import lax
from jax.experimental import pallas as pl
from jax.experimental.pallas import tpu as pltpu

F32 = jnp.float32
BF16 = jnp.bfloat16

RMS_EPS = 1e-6
L2_EPS = 1e-6
LRU_C = 8.0
CONV_WIDTH = 4
GDN_CHUNK = 64
INV_BLOCK = 8
GDN_PACK = 4
CONV_COLS = 256
LANES = 128
SUBLANES = 8
BF16_ROWS = 16
VMEM_LIMIT = 56 * 1024 * 1024


def _mm(a, b):
    return jnp.dot(a.astype(BF16), b.astype(BF16), preferred_element_type=F32)


def _mm_tn(a, b):
    return lax.dot_general(a.astype(BF16), b.astype(BF16), (((0,), (0,)), ((), ())),
                           preferred_element_type=F32)


def _softplus(x):
    return jnp.maximum(x, 0.0) + jnp.log1p(jnp.exp(-jnp.abs(x)))


def _silu(x):
    h = 0.5 * x
    return h + h * jnp.tanh(h)


def _gelu_tanh(x):
    c = 0.7978845608028654
    return 0.5 * x * (1.0 + jnp.tanh(c * (x + 0.044715 * (x * x * x))))


def _cast_plan(w, steps):
    rows = w.shape[1]
    units = rows // BF16_ROWS
    assert rows % BF16_ROWS == 0
    nb = max(k for k in range(1, min(units, steps) + 1) if units % k == 0)
    return nb, rows // nb


def _cast_specs(w, layer, steps, step_of):
    nb, rb = _cast_plan(w, steps)
    cols = w.shape[2]
    blk = lambda *ids: jnp.minimum(step_of(*ids), nb - 1)
    return (pl.BlockSpec((None, rb, cols), lambda *ids: (layer, blk(*ids), 0)),
            pl.BlockSpec((rb, cols), lambda *ids: (blk(*ids), 0)),
            jax.ShapeDtypeStruct((w.shape[1], cols), BF16))


def _cast_step(step, srcs, dsts, nbs):
    for src, dst, nb in zip(srcs, dsts, nbs):
        @pl.when(step < nb)
        def _(src=src, dst=dst):
            dst[...] = src[...].astype(dst.dtype)


def _dot_nt(a, b):
    return lax.dot_general(a, b, (((1,), (1,)), ((), ())), preferred_element_type=F32)


def _in_proj_kernel(x_ref, g_ref, wa_ref, wb_ref, wab_ref, cw_ref, cb_ref, c0_ref, c1_ref, c2_ref,
                    o_ref, oab_ref, d0_ref, d1_ref, d2_ref, h_ref, tail_ref, *,
                    tiles_per_seq, lru_step, cast_blocks):
    i = pl.program_id(0)
    j = pl.program_id(1)
    tm, tn = o_ref.shape
    half = tn // 2
    _cast_step(i * pl.num_programs(1) + j, (c0_ref, c1_ref, c2_ref), (d0_ref, d1_ref, d2_ref), cast_blocks)

    @pl.when(j == 0)
    def _():
        x = x_ref[...]
        ms = jnp.mean(x * x, axis=-1, keepdims=True)
        h = (x * lax.rsqrt(ms + RMS_EPS) * g_ref[...]).astype(BF16)
        h_ref[...] = h
        oab_ref[...] = _dot_nt(h, wab_ref[...])

        @pl.when(i == 0)
        def _():
            tail_ref[...] = jnp.zeros_like(tail_ref)

    cc = min(half, CONV_COLS)
    row8 = lax.broadcasted_iota(jnp.int32, (SUBLANES, cc), 0)

    def shift_rows(val, prev8, s):
        rolled = pltpu.roll(val, s, axis=0)
        top = jnp.where(row8 < s, pltpu.roll(prev8, s, axis=0), rolled[0:SUBLANES, :])
        return jnp.concatenate([top, rolled[SUBLANES:, :]], axis=0)

    def conv_chunks(finish):
        first = i % tiles_per_seq == 0
        nb = half // cc
        for ci in range(nb):
            c0 = ci * cc
            col = pl.multiple_of(j * half, half) + c0
            prev = jnp.where(first, 0.0, tail_ref[:, pl.ds(col, cc)])
            res = _dot_nt(h_ref[...], wa_ref[c0:c0 + cc, :])
            tail_ref[:, pl.ds(col, cc)] = res[tm - SUBLANES:tm, :]
            b0 = half + c0
            o_ref[:, b0:b0 + cc] = _dot_nt(h_ref[...], wb_ref[c0:c0 + cc, :]).astype(o_ref.dtype)
            taps = [cw_ref[k:k + 1, c0:c0 + cc] for k in range(CONV_WIDTH)]
            s1 = shift_rows(res, prev, 1)
            u = res * taps[3] + s1 * taps[2]
            v = res * taps[1] + s1 * taps[0]
            prev_v = prev * taps[1] + pltpu.roll(prev, 1, axis=0) * taps[0]
            finish(u + shift_rows(v, prev_v, 2), c0)

    @pl.when(j < 3)
    def _():
        is_v = j == 2
        q_scale = jnp.where(j == 0, LANES ** -0.5, 1.0)

        def finish(acc, c0):
            y = _silu(acc)
            for g in range(cc // LANES):
                yg = y[:, g * LANES:(g + 1) * LANES]
                inv = lax.rsqrt(jnp.sum(yg * yg, axis=-1, keepdims=True) + L2_EPS)
                o_ref[:, c0 + g * LANES:c0 + (g + 1) * LANES] = (
                    yg * jnp.where(is_v, 1.0, inv * q_scale)).astype(o_ref.dtype)

        conv_chunks(finish)

    @pl.when(j == lru_step)
    def _():
        def finish(acc, c0):
            o_ref[:, c0:c0 + cc] = (acc + cb_ref[:, c0:c0 + cc]).astype(o_ref.dtype)

        conv_chunks(finish)

    @pl.when((j >= 3) & (j != lru_step))
    def _():
        o_ref[:, 0:half] = _dot_nt(h_ref[...], wa_ref[...]).astype(o_ref.dtype)
        o_ref[:, half:tn] = _dot_nt(h_ref[...], wb_ref[...]).astype(o_ref.dtype)


def _in_proj(x2, gain, w_t, conv_w, conv_b, casts, layer, tm, half, seq, lru_step, a_rows, b_rows, ab_row):
    m, d = x2.shape
    nj = len(a_rows)
    tn = 2 * half
    n = nj * tn
    steps = (m // tm) * nj
    cast = [_cast_specs(w, layer, steps, lambda i, j: i * nj + j) for w in casts]
    kern = functools.partial(_in_proj_kernel, tiles_per_seq=seq // tm, lru_step=lru_step,
                             cast_blocks=tuple(_cast_plan(w, steps)[0] for w in casts))

    def rows_of(table):
        return lambda i, j: (layer, BF16_ROWS * sum(jnp.where(j == s, r // BF16_ROWS, 0)
                                                     for s, r in enumerate(table)), 0)

    w_tile = (None, pl.Element(half), pl.Element(d))
    return pl.pallas_call(
        kern,
        grid=(m // tm, nj),
        in_specs=[
            pl.BlockSpec((tm, d), lambda i, j: (i, 0)),
            pl.BlockSpec((None, 1, d), lambda i, j: (layer, 0, 0)),
            pl.BlockSpec(w_tile, rows_of(a_rows)),
            pl.BlockSpec(w_tile, rows_of(b_rows)),
            pl.BlockSpec((None, pl.Element(LANES), pl.Element(d)), lambda i, j: (layer, ab_row, 0)),
            pl.BlockSpec((None, CONV_WIDTH, half), lambda i, j: (layer, 0, j)),
            pl.BlockSpec((None, 1, half), lambda i, j: (layer, 0, j)),
        ] + [c[0] for c in cast],
        out_specs=[
            pl.BlockSpec((tm, tn), lambda i, j: (i, j)),
            pl.BlockSpec((tm, LANES), lambda i, j: (i, 0)),
        ] + [c[1] for c in cast],
        out_shape=[
            jax.ShapeDtypeStruct((m, n), BF16),
            jax.ShapeDtypeStruct((m, LANES), F32),
        ] + [c[2] for c in cast],
        scratch_shapes=[pltpu.VMEM((tm, d), BF16), pltpu.VMEM((SUBLANES, n // 2), F32)],
        compiler_params=pltpu.CompilerParams(
            dimension_semantics=("arbitrary", "arbitrary"), vmem_limit_bytes=VMEM_LIMIT),
        name="in_proj",
    )(x2, gain, w_t, w_t, w_t, conv_w, conv_b, *casts)


def _gdn_kernel(q_ref, k_ref, v_ref, z_ref, ab_ref, alog_ref, dtb_ref, gain_ref, o_ref,
                state, qk_s, rhs_s, u_s, w_s, qd_s, kd_s, *, heads):
    hd = LANES
    width = heads * hd
    c = GDN_CHUNK
    ts = GDN_PACK * c
    hs = range(heads)

    @pl.when(pl.program_id(1) == 0)
    def _():
        state[...] = jnp.zeros_like(state)

    ab = ab_ref[...]
    g_all = -jnp.exp(alog_ref[...]) * _softplus(ab + dtb_ref[...])
    beta_all = jax.nn.sigmoid(ab)

    r2 = lax.broadcasted_iota(jnp.int32, (ts, ts), 0)
    c2 = lax.broadcasted_iota(jnp.int32, (ts, ts), 1)
    same_chunk = (r2 & -c) == (c2 & -c)
    causal_bd = same_chunk & (r2 >= c2)
    strict_bd = same_chunk & (r2 > c2)
    rp = lax.broadcasted_iota(jnp.int32, (c, ts), 0)
    cp = lax.broadcasted_iota(jnp.int32, (c, ts), 1) & (c - 1)
    eye_p = jnp.where(rp == cp, 1.0, 0.0)
    in_diag_block = (rp & -INV_BLOCK) == (cp & -INV_BLOCK)

    lane = lax.broadcasted_iota(jnp.int32, (c, hd), 1)
    lane_masks = [jnp.where((lane & -c) == p * c, 1.0, 0.0).astype(BF16) for p in range(hd // c)]
    zero_blk = jnp.zeros((c, hd), BF16)

    def blockdiag(y):
        yb = y.astype(BF16)
        rows = []
        for i in range(GDN_PACK):
            slab, off = divmod(i * c, hd)
            blocks = [zero_blk] * (ts // hd)
            blocks[slab] = yb[:, slab * hd:(slab + 1) * hd] * lane_masks[off // c]
            rows.append(jnp.concatenate(blocks, axis=1))
        return jnp.concatenate(rows, axis=0)

    def bmm(x, bd):
        return jnp.dot(x.astype(BF16), bd, preferred_element_type=F32)

    g_hi = g_all.astype(BF16)
    r_mid = g_all - g_hi.astype(F32)
    g_mid = r_mid.astype(BF16)
    g_lo = (r_mid - g_mid.astype(F32)).astype(BF16)
    sums = jnp.dot(jnp.where(causal_bd, 1.0, 0.0).astype(BF16), jnp.concatenate([g_hi, g_mid, g_lo], axis=1),
                   preferred_element_type=F32)
    gcum = sums[:, 0:LANES] + sums[:, LANES:2 * LANES] + sums[:, 2 * LANES:3 * LANES]
    glast = jnp.concatenate([jnp.broadcast_to(gcum[(ci + 1) * c - 1:(ci + 1) * c, :], (c, LANES))
                             for ci in range(GDN_PACK)], axis=0)
    gcum_t = gcum.T

    prods = []
    for h in hs:
        qb = q_ref[:, h * hd:(h + 1) * hd]
        kb16 = k_ref[:, h * hd:(h + 1) * hd]
        q = qb.astype(F32)
        k = kb16.astype(F32)
        v = v_ref[:, h * hd:(h + 1) * hd].astype(F32)
        gcol = gcum[:, h:h + 1]
        egc = jnp.exp(gcol)
        beta = beta_all[:, heads + h:heads + h + 1]
        kb = k * beta
        prods.append(lax.dot_general(jnp.concatenate([qb, kb.astype(BF16)], axis=0), kb16,
                                     (((1,), (1,)), ((), ())), preferred_element_type=F32))
        rhs_s[h] = jnp.concatenate([v * beta, kb * egc], axis=1).astype(BF16)
        qd_s[h] = (q * egc).astype(BF16)
        kd_s[h] = (k * jnp.exp(glast[:, h:h + 1] - gcol)).astype(BF16)
    lps = []
    for h in hs:
        gcol = gcum[:, h:h + 1]
        grow = gcum_t[h:h + 1, :]
        dmat = jnp.where(causal_bd, jnp.exp(jnp.where(causal_bd, gcol - grow, 0.0)), 0.0)
        qk_s[h] = (prods[h][0:ts] * dmat).astype(BF16)
        lbd = jnp.where(strict_bd, prods[h][ts:2 * ts] * dmat, 0.0)
        lp = lbd[0:c]
        for i in range(1, GDN_PACK):
            lp = lp + lbd[i * c:(i + 1) * c]
        lps.append(lp)

    ld = [jnp.where(in_diag_block, lp, 0.0) for lp in lps]
    tinv = [eye_p - d for d in ld]
    pw = [bmm(d, blockdiag(d)) for d in ld]
    span = 2
    while span < INV_BLOCK:
        last = 2 * span >= INV_BLOCK
        res = [bmm(d if last else jnp.concatenate([p, d], axis=0), blockdiag(p)) for p, d in zip(pw, tinv)]
        if last:
            tinv = [d + r for d, r in zip(tinv, res)]
        else:
            pw = [r[0:c] for r in res]
            tinv = [d + r[c:2 * c] for d, r in zip(tinv, res)]
        span *= 2
    b = INV_BLOCK
    while b < c:
        off = ((rp & -2 * b) == (cp & -2 * b)) & ((rp & -b) != (cp & -b))
        te = [bmm(x, blockdiag(jnp.where(off, lp, 0.0))) for x, lp in zip(tinv, lps)]
        tinv = [x - bmm(y, blockdiag(x)) for x, y in zip(tinv, te)]
        b *= 2

    for h in hs:
        uw = jnp.dot(blockdiag(tinv[h]), rhs_s[h], preferred_element_type=F32)
        u_s[h] = uw[:, 0:hd]
        w_s[h] = uw[:, hd:2 * hd].astype(BF16)

    gain = gain_ref[...]
    zero_v = jnp.zeros((c, hd), BF16)
    for ci in range(GDN_PACK):
        r0 = ci * c
        slab = (r0 // hd) * hd
        wss = [_mm(jnp.concatenate([w_s[h, r0:r0 + c, :], qd_s[h, r0:r0 + c, :]], axis=0), state[h]) for h in hs]
        for h in hs:
            v_new = u_s[h, r0:r0 + c, :] - wss[h][0:c]
            v_new_b = v_new.astype(BF16)
            pieces = [zero_v] * (hd // c)
            pieces[(r0 - slab) // c] = v_new_b
            out = wss[h][c:2 * c] + jnp.dot(qk_s[h, r0:r0 + c, slab:slab + hd], jnp.concatenate(pieces, axis=0),
                                            preferred_element_type=F32)
            state[h] = (state[h] * jnp.exp(glast[r0:r0 + 1, h:h + 1])
                        + _mm_tn(kd_s[h, r0:r0 + c, :], v_new_b))
            o = out * lax.rsqrt(jnp.mean(out * out, axis=-1, keepdims=True) + RMS_EPS) * gain
            zz = z_ref[r0:r0 + c, h * hd:(h + 1) * hd].astype(F32)
            o_ref[r0:r0 + c, h * hd:(h + 1) * hd] = (o * _silu(zz)).astype(o_ref.dtype)


def _gdn(proj, ab, alog_row, dtb_row, gain_row, layer, batch, seq, heads, cols):
    width = heads * LANES
    ts = GDN_PACK * GDN_CHUNK
    nt = seq // ts
    kern = functools.partial(_gdn_kernel, heads=heads)
    return pl.pallas_call(
        kern,
        grid=(batch, nt),
        in_specs=[
            pl.BlockSpec((ts, width), lambda b, t: (b * nt + t, cols[0])),
            pl.BlockSpec((ts, width), lambda b, t: (b * nt + t, cols[1])),
            pl.BlockSpec((ts, width), lambda b, t: (b * nt + t, cols[2])),
            pl.BlockSpec((ts, width), lambda b, t: (b * nt + t, cols[3])),
            pl.BlockSpec((ts, LANES), lambda b, t: (b * nt + t, 0)),
            pl.BlockSpec((None, 1, LANES), lambda b, t: (layer, 0, 0)),
            pl.BlockSpec((None, 1, LANES), lambda b, t: (layer, 0, 0)),
            pl.BlockSpec((None, 1, LANES), lambda b, t: (layer, 0, 0)),
        ],
        out_specs=pl.BlockSpec((ts, width), lambda b, t: (b * nt + t, 0)),
        out_shape=jax.ShapeDtypeStruct((batch * seq, width), BF16),
        scratch_shapes=[
            pltpu.VMEM((heads, LANES, LANES), F32),
            pltpu.VMEM((heads, ts, ts), BF16),
            pltpu.VMEM((heads, ts, 2 * LANES), BF16),
            pltpu.VMEM((heads, ts, LANES), F32),
            pltpu.VMEM((heads, ts, LANES), BF16),
            pltpu.VMEM((heads, ts, LANES), BF16),
            pltpu.VMEM((heads, ts, LANES), BF16),
        ],
        compiler_params=pltpu.CompilerParams(
            dimension_semantics=("arbitrary", "arbitrary"), vmem_limit_bytes=VMEM_LIMIT),
        name="gdn",
    )(proj, proj, proj, proj, ab, alog_row, dtb_row, gain_row)


def _lru_kernel(xc_ref, yb_ref, wa_ref, ba_ref, wx_ref, bx_ref, lam_ref, o_ref,
                abuf, bbuf, hcar, *, blocks, ts):
    bd = LANES
    width = blocks * bd

    @pl.when(pl.program_id(1) == 0)
    def _():
        hcar[...] = jnp.zeros_like(hcar)

    for n in range(blocks):
        lo = n * bd
        xcb = xc_ref[:, lo:lo + bd]
        xc = xcb.astype(F32)
        r = jax.nn.sigmoid(jnp.dot(xcb, wa_ref[n], preferred_element_type=F32) + ba_ref[0:1, lo:lo + bd])
        i = jax.nn.sigmoid(jnp.dot(xcb, wx_ref[n], preferred_element_type=F32) + bx_ref[0:1, lo:lo + bd])
        log_a = (-LRU_C) * r * _softplus(-lam_ref[0:1, lo:lo + bd])
        a = jnp.exp(log_a)
        one_minus_a2 = -jnp.tanh(log_a) * (1.0 + a * a)
        abuf[:, lo:lo + bd] = a
        bbuf[:, lo:lo + bd] = jnp.sqrt(one_minus_a2) * (i * xc)

    rowi = lax.broadcasted_iota(jnp.int32, (SUBLANES, width), 0)

    def group(gi, hprev):
        r0 = pl.multiple_of(gi * SUBLANES, SUBLANES)
        a = abuf[pl.ds(r0, SUBLANES), :]
        b = bbuf[pl.ds(r0, SUBLANES), :]
        s = 1
        while s < SUBLANES:
            keep = rowi >= s
            a_sh = jnp.where(keep, pltpu.roll(a, s, axis=0), 1.0)
            b_sh = jnp.where(keep, pltpu.roll(b, s, axis=0), 0.0)
            b = a * b_sh + b
            a = a * a_sh
            s *= 2
        bbuf[pl.ds(r0, SUBLANES), :] = a * hprev + b
        return a[SUBLANES - 1:SUBLANES, :] * hprev + b[SUBLANES - 1:SUBLANES, :]

    hcar[...] = lax.fori_loop(0, ts // SUBLANES, group, hcar[...], unroll=4)
    o_ref[...] = (bbuf[...] * _gelu_tanh(yb_ref[...].astype(F32))).astype(o_ref.dtype)


def _lru(proj, w_a, b_a, w_x, b_x, lam, layer, batch, seq, blocks, xb_col, yb_col, ts):
    width = blocks * LANES
    nt = seq // ts
    kern = functools.partial(_lru_kernel, blocks=blocks, ts=ts)
    row = lambda b, t: (layer, 0, 0)
    mat = lambda b, t: (layer, 0, 0, 0)
    return pl.pallas_call(
        kern,
        grid=(batch, nt),
        in_specs=[
            pl.BlockSpec((ts, width), lambda b, t: (b * nt + t, xb_col)),
            pl.BlockSpec((ts, width), lambda b, t: (b * nt + t, yb_col)),
            pl.BlockSpec((None, blocks, LANES, LANES), mat),
            pl.BlockSpec((None, 1, width), row),
            pl.BlockSpec((None, blocks, LANES, LANES), mat),
            pl.BlockSpec((None, 1, width), row),
            pl.BlockSpec((None, 1, width), row),
        ],
        out_specs=pl.BlockSpec((ts, width), lambda b, t: (b * nt + t, 0)),
        out_shape=jax.ShapeDtypeStruct((batch * seq, width), BF16),
        scratch_shapes=[
            pltpu.VMEM((ts, width), F32),
            pltpu.VMEM((ts, width), F32),
            pltpu.VMEM((1, width), F32),
        ],
        compiler_params=pltpu.CompilerParams(
            dimension_semantics=("arbitrary", "arbitrary"), vmem_limit_bytes=VMEM_LIMIT),
        name="lru",
    )(proj, proj, w_a, b_a, w_x, b_x, lam)


def _merge_kernel(x_ref, og_ref, ol_ref, gg0_ref, gg1_ref, gl_ref, wg_ref, wl_ref, wo_ref, c0_ref, c1_ref,
                  o_ref, d0_ref, d1_ref, *, cast_blocks):
    _cast_step(pl.program_id(0), (c0_ref, c1_ref), (d0_ref, d1_ref), cast_blocks)
    pg = jnp.dot(og_ref[...], wg_ref[...], preferred_element_type=F32)
    pb = jnp.dot(ol_ref[...], wl_ref[...], preferred_element_type=F32)
    gg = jnp.concatenate([gg0_ref[...], gg1_ref[...]], axis=1)
    merged = (jax.nn.sigmoid(gg.astype(F32)) * pg
              + jax.nn.sigmoid(gl_ref[...].astype(F32)) * pb)
    o_ref[...] = x_ref[...] + jnp.dot(merged.astype(BF16), wo_ref[...], preferred_element_type=F32)


def _merge(x2, o_gdn, o_lru, proj, w_bg, w_bl, w_out, casts, layer, gg_cols, gl_col, tm):
    m, d = x2.shape
    wg = o_gdn.shape[1]
    wl = o_lru.shape[1]
    const = lambda i: (0, 0)
    steps = m // tm
    cast = [_cast_specs(w, layer, steps, lambda i: i) for w in casts]
    kern = functools.partial(_merge_kernel, cast_blocks=tuple(_cast_plan(w, steps)[0] for w in casts))
    return pl.pallas_call(
        kern,
        grid=(steps,),
        in_specs=[
            pl.BlockSpec((tm, d), lambda i: (i, 0)),
            pl.BlockSpec((tm, wg), lambda i: (i, 0)),
            pl.BlockSpec((tm, wl), lambda i: (i, 0)),
            pl.BlockSpec((tm, d // 2), lambda i: (i, gg_cols[0])),
            pl.BlockSpec((tm, d // 2), lambda i: (i, gg_cols[1])),
            pl.BlockSpec((tm, d), lambda i: (i, gl_col)),
            pl.BlockSpec((wg, d), const, pipeline_mode=pl.Buffered(1)),
            pl.BlockSpec((wl, d), const, pipeline_mode=pl.Buffered(1)),
            pl.BlockSpec((d, d), const, pipeline_mode=pl.Buffered(1)),
        ] + [c[0] for c in cast],
        out_specs=[pl.BlockSpec((tm, d), lambda i: (i, 0))] + [c[1] for c in cast],
        out_shape=[jax.ShapeDtypeStruct((m, d), F32)] + [c[2] for c in cast],
        compiler_params=pltpu.CompilerParams(
            dimension_semantics=("arbitrary",), vmem_limit_bytes=VMEM_LIMIT),
        name="merge",
    )(x2, o_gdn, o_lru, proj, proj, proj, w_bg, w_bl, w_out, *casts)


def _mlp_kernel(x_ref, g_ref, wu_ref, wd_ref, fg_ref, o_ref, h_ref, *, final_norm):
    f = pl.program_id(1)

    @pl.when(f == 0)
    def _():
        x = x_ref[...]
        ms = jnp.mean(x * x, axis=-1, keepdims=True)
        h_ref[...] = (x * lax.rsqrt(ms + RMS_EPS) * g_ref[...]).astype(BF16)
        o_ref[...] = x

    u = jnp.dot(h_ref[...], wu_ref[...], preferred_element_type=F32)
    u = jnp.maximum(u, 0.0)
    o_ref[...] += jnp.dot((u * u).astype(BF16), wd_ref[...], preferred_element_type=F32)

    if final_norm:
        @pl.when(f == pl.num_programs(1) - 1)
        def _():
            y = o_ref[...]
            ms = jnp.mean(y * y, axis=-1, keepdims=True)
            o_ref[...] = y * lax.rsqrt(ms + RMS_EPS) * fg_ref[...]


def _mlp(x2, gain, w_up, w_down, final_gain, layer, final_norm, tm, tf):
    m, d = x2.shape
    ff = w_up.shape[1]
    kern = functools.partial(_mlp_kernel, final_norm=final_norm)
    return pl.pallas_call(
        kern,
        grid=(m // tm, ff // tf),
        in_specs=[
            pl.BlockSpec((tm, d), lambda i, f: (i, 0)),
            pl.BlockSpec((None, 1, d), lambda i, f: (layer, 0, 0)),
            pl.BlockSpec((d, tf), lambda i, f: (0, f)),
            pl.BlockSpec((tf, d), lambda i, f: (f, 0)),
            pl.BlockSpec((1, d), lambda i, f: (0, 0)),
        ],
        out_specs=pl.BlockSpec((tm, d), lambda i, f: (i, 0)),
        out_shape=jax.ShapeDtypeStruct((m, d), F32),
        scratch_shapes=[pltpu.VMEM((tm, d), BF16)],
        compiler_params=pltpu.CompilerParams(
            dimension_semantics=("arbitrary", "arbitrary"), vmem_limit_bytes=VMEM_LIMIT),
        name="mlp",
    )(x2, gain, w_up, w_down, final_gain)


def _tile(n, want):
    t = min(want, n)
    assert n % t == 0, (n, want)
    return t


def kernel(x, attn_norm, w_in, gdn_conv_w, gdn_a_log, gdn_dt_bias, gdn_norm, lru_conv_w, lru_conv_b,
           lru_w_a, lru_b_a, lru_w_x, lru_b_x, lru_lambda, w_branch_gdn, w_branch_lru, w_out,
           mlp_norm, w_up, w_down, final_norm):
    batch, seq, d = x.shape
    depth = w_in.shape[0]
    heads = gdn_a_log.shape[1]
    gw = heads * LANES
    assert gdn_norm.shape[1] == LANES and gdn_conv_w.shape[2] == 3 * gw
    blocks = lru_w_a.shape[1]
    lw = blocks * LANES
    assert lru_w_a.shape[2] == LANES
    assert 2 * heads <= LANES and seq % (GDN_PACK * GDN_CHUNK) == 0
    assert gw == lw
    m = batch * seq

    o_z = 3 * gw
    o_a = o_z + gw
    o_xb = o_a + 2 * heads
    o_yb = o_xb + lw
    o_gg = o_yb + lw
    o_gl = o_gg + d
    assert w_in.shape[2] == o_gl + d
    assert d == 2 * gw
    a_rows = (0, gw, 2 * gw, o_xb, o_gl)
    b_rows = (o_z, o_yb, o_gg, o_gg + gw, o_gl + gw)
    assert all(r % BF16_ROWS == 0 for r in a_rows + b_rows + (o_a,))
    gdn_cols = (0, 2, 4, 1)
    yb_col, xb_col = 3, 6
    gg_cols, gl_col = (5, 7), 4
    lru_step = 3

    tm_in = _tile(seq, 1024)
    ts_lru = _tile(seq, 512)
    tm_merge = _tile(m, 256)
    tm_mlp = _tile(m, 1024)
    tf_mlp = _tile(w_up.shape[2], 512)

    w_t = jnp.swapaxes(w_in, 1, 2).astype(BF16)
    zeros_w = jnp.zeros((depth, CONV_WIDTH, gw), F32)
    conv_w = jnp.concatenate([gdn_conv_w.astype(F32), lru_conv_w.astype(F32), zeros_w], axis=2)
    conv_b = jnp.pad(lru_conv_b.astype(F32), ((0, 0), (3 * gw, gw)))[:, None, :]
    pad_h = ((0, 0), (0, LANES - heads))
    alog_rows = jnp.pad(gdn_a_log.astype(F32), pad_h)[:, None, :]
    dtb_rows = jnp.pad(gdn_dt_bias.astype(F32), pad_h)[:, None, :]
    w_a_b, w_x_b = lru_w_a.astype(BF16), lru_w_x.astype(BF16)
    row3 = lambda v: v[:, None, :]

    x2 = x.reshape(m, d)
    for l in range(depth):
        proj, ab, w_bg_b, w_bl_b, w_out_b = _in_proj(
            x2, row3(attn_norm), w_t, conv_w, conv_b, (w_branch_gdn, w_branch_lru, w_out),
            l, tm_in, gw, seq, lru_step, a_rows, b_rows, o_a)
        o_gdn = _gdn(proj, ab, alog_rows, dtb_rows, row3(gdn_norm), l, batch, seq, heads, gdn_cols)
        o_lru = _lru(proj, w_a_b, row3(lru_b_a), w_x_b, row3(lru_b_x),
                     row3(lru_lambda), l, batch, seq, blocks, xb_col, yb_col, ts_lru)
        x2, w_up_b, w_down_b = _merge(x2, o_gdn, o_lru, proj, w_bg_b, w_bl_b, w_out_b, (w_up, w_down),
                                      l, gg_cols, gl_col, tm_merge)
        x2 = _mlp(x2, row3(mlp_norm), w_up_b, w_down_b, final_norm.reshape(1, d), l, l == depth - 1,
                  tm_mlp, tf_mlp)
    if depth == 0:
        raise ValueError("depth must be positive")
    return x2.reshape(batch, seq, d)
```

```python
import functools

import jax
import jax.numpy as jnp
from jax import lax
from jax.experimental import pallas as pl
from jax.experimental.pallas import tpu as pltpu

F32 = jnp.float32
BF16 = jnp.bfloat16

RMS_EPS = 1e-6
L2_EPS = 1e-6
LRU_C = 8.0
CONV_WIDTH = 4
GDN_CHUNK = 64
INV_BLOCK = 8
GDN_HEAD_GROUP = 8
GDN_PACK = 4
CONV_COLS = 256
LANES = 128
SUBLANES = 8
BF16_ROWS = 16
VMEM_LIMIT = 60 * 1024 * 1024


def _mm(a, b):
    return jnp.dot(a.astype(BF16), b.astype(BF16), preferred_element_type=F32)


def _mm_tn(a, b):
    return lax.dot_general(a.astype(BF16), b.astype(BF16), (((0,), (0,)), ((), ())),
                           preferred_element_type=F32)


def _softplus(x):
    return jnp.maximum(x, 0.0) + jnp.log1p(jnp.exp(-jnp.abs(x)))


def _sigmoid(x):
    return 0.5 * jnp.tanh(0.5 * x) + 0.5


def _silu(x):
    h = 0.5 * x
    return h + h * jnp.tanh(h)


def _gelu_tanh(x):
    c = 0.7978845608028654
    return 0.5 * x * (1.0 + jnp.tanh(c * (x + 0.044715 * (x * x * x))))


def _cast_plan(w, steps):
    rows = w.shape[1]
    units = rows // BF16_ROWS
    assert rows % BF16_ROWS == 0
    nb = max(k for k in range(1, min(units, steps) + 1) if units % k == 0)
    return nb, rows // nb


def _cast_specs(w, layer, steps, step_of):
    nb, rb = _cast_plan(w, steps)
    cols = w.shape[2]
    blk = lambda *ids: jnp.minimum(step_of(*ids), nb - 1)
    return (pl.BlockSpec((None, rb, cols), lambda *ids: (layer, blk(*ids), 0)),
            pl.BlockSpec((rb, cols), lambda *ids: (blk(*ids), 0)),
            jax.ShapeDtypeStruct((w.shape[1], cols), BF16))


def _cast_step(srcs, dsts):
    for src, dst in zip(srcs, dsts):
        dst[...] = src[...].astype(dst.dtype)


def _dot_nt(a, b):
    return lax.dot_general(a, b, (((1,), (1,)), ((), ())), preferred_element_type=F32)


def _in_proj_kernel(x_ref, g_ref, wa_ref, wb_ref, wab_ref, cw_ref, cb_ref, c0_ref, c1_ref, c2_ref,
                    o_ref, oab_ref, d0_ref, d1_ref, d2_ref, h_ref, tail_ref, *,
                    tiles_per_seq, lru_step):
    i = pl.program_id(0)
    j = pl.program_id(1)
    tm, tn = o_ref.shape
    half = tn // 2

    def ride_casts():
        _cast_step((c0_ref, c1_ref, c2_ref), (d0_ref, d1_ref, d2_ref))

    @pl.when(j == 0)
    def _():
        x = x_ref[...]
        ms = jnp.mean(x * x, axis=-1, keepdims=True)
        h = (x * lax.rsqrt(ms + RMS_EPS) * g_ref[...]).astype(BF16)
        h_ref[...] = h
        oab_ref[...] = _dot_nt(h, wab_ref[...])

        @pl.when(i == 0)
        def _():
            tail_ref[...] = jnp.zeros_like(tail_ref)

    cc = min(half, CONV_COLS)
    row8 = lax.broadcasted_iota(jnp.int32, (SUBLANES, cc), 0)

    def shift_rows(val, prev8, s):
        rolled = pltpu.roll(val, s, axis=0)
        top = jnp.where(row8 < s, pltpu.roll(prev8, s, axis=0), rolled[0:SUBLANES, :])
        return jnp.concatenate([top, rolled[SUBLANES:, :]], axis=0)

    def conv_chunks(finish):
        ride_casts()
        first = i % tiles_per_seq == 0
        nb = half // cc
        for ci in range(nb):
            c0 = ci * cc
            col = pl.multiple_of(j * half, half) + c0
            prev = jnp.where(first, 0.0, tail_ref[:, pl.ds(col, cc)])
            res = _dot_nt(h_ref[...], wa_ref[c0:c0 + cc, :])
            tail_ref[:, pl.ds(col, cc)] = res[tm - SUBLANES:tm, :]
            b0 = half + c0
            o_ref[:, b0:b0 + cc] = _dot_nt(h_ref[...], wb_ref[c0:c0 + cc, :]).astype(o_ref.dtype)
            taps = [cw_ref[k:k + 1, c0:c0 + cc] for k in range(CONV_WIDTH)]
            s1 = shift_rows(res, prev, 1)
            u = res * taps[3] + s1 * taps[2]
            v = res * taps[1] + s1 * taps[0]
            prev_v = prev * taps[1] + pltpu.roll(prev, 1, axis=0) * taps[0]
            finish(u + shift_rows(v, prev_v, 2), c0)

    @pl.when(j < 3)
    def _():
        is_v = j == 2
        q_scale = jnp.where(j == 0, LANES ** -0.5, 1.0)

        def finish(acc, c0):
            y = _silu(acc)
            for g in range(cc // LANES):
                yg = y[:, g * LANES:(g + 1) * LANES]
                inv = lax.rsqrt(jnp.sum(yg * yg, axis=-1, keepdims=True) + L2_EPS)
                o_ref[:, c0 + g * LANES:c0 + (g + 1) * LANES] = (
                    yg * jnp.where(is_v, 1.0, inv * q_scale)).astype(o_ref.dtype)

        conv_chunks(finish)

    @pl.when(j == lru_step)
    def _():
        def finish(acc, c0):
            o_ref[:, c0:c0 + cc] = (acc + cb_ref[:, c0:c0 + cc]).astype(o_ref.dtype)

        conv_chunks(finish)

    @pl.when((j >= 3) & (j != lru_step))
    def _():
        ride_casts()
        o_ref[:, 0:half] = _dot_nt(h_ref[...], wa_ref[...]).astype(o_ref.dtype)
        o_ref[:, half:tn] = _dot_nt(h_ref[...], wb_ref[...]).astype(o_ref.dtype)


def _in_proj(x2, gain, w_t, conv_w, conv_b, casts, layer, tm, half, seq, lru_step, a_rows, b_rows, ab_row):
    m, d = x2.shape
    nj = len(a_rows)
    tn = 2 * half
    n = nj * tn
    steps = (m // tm) * nj
    cast = [_cast_specs(w, layer, steps, lambda i, j: i * nj + j) for w in casts]
    kern = functools.partial(_in_proj_kernel, tiles_per_seq=seq // tm, lru_step=lru_step)

    def rows_of(table):
        return lambda i, j: (layer, BF16_ROWS * sum(jnp.where(j == s, r // BF16_ROWS, 0)
                                                     for s, r in enumerate(table)), 0)

    w_tile = (None, pl.Element(half), pl.Element(d))
    return pl.pallas_call(
        kern,
        grid=(m // tm, nj),
        in_specs=[
            pl.BlockSpec((tm, d), lambda i, j: (i, 0)),
            pl.BlockSpec((None, 1, d), lambda i, j: (layer, 0, 0)),
            pl.BlockSpec(w_tile, rows_of(a_rows)),
            pl.BlockSpec(w_tile, rows_of(b_rows)),
            pl.BlockSpec((None, pl.Element(LANES), pl.Element(d)), lambda i, j: (layer, ab_row, 0)),
            pl.BlockSpec((None, CONV_WIDTH, half), lambda i, j: (layer, 0, j)),
            pl.BlockSpec((None, 1, half), lambda i, j: (layer, 0, j)),
        ] + [c[0] for c in cast],
        out_specs=[
            pl.BlockSpec((tm, tn), lambda i, j: (i, j)),
            pl.BlockSpec((tm, LANES), lambda i, j: (i, 0)),
        ] + [c[1] for c in cast],
        out_shape=[
            jax.ShapeDtypeStruct((m, n), BF16),
            jax.ShapeDtypeStruct((m, LANES), F32),
        ] + [c[2] for c in cast],
        scratch_shapes=[pltpu.VMEM((tm, d), BF16), pltpu.VMEM((SUBLANES, n // 2), F32)],
        compiler_params=pltpu.CompilerParams(
            dimension_semantics=("arbitrary", "arbitrary"), vmem_limit_bytes=VMEM_LIMIT),
        name="in_proj",
    )(x2, gain, w_t, w_t, w_t, conv_w, conv_b, *casts)


def _gdn_kernel(q_ref, k_ref, v_ref, z_ref, ab_ref, alog_ref, dtb_ref, gain_ref, o_ref,
                state, qk_s, rhs_s, u_s, w_s, qd_s, kd_s, *, heads):
    hd = LANES
    width = heads * hd
    c = GDN_CHUNK
    ts = GDN_PACK * c
    hs = range(heads)

    @pl.when(pl.program_id(1) == 0)
    def _():
        state[...] = jnp.zeros_like(state)

    ab = ab_ref[...]
    g_all = -jnp.exp(alog_ref[...]) * _softplus(ab + dtb_ref[...])
    beta_all = _sigmoid(ab)

    r2 = lax.broadcasted_iota(jnp.int32, (ts, ts), 0)
    c2 = lax.broadcasted_iota(jnp.int32, (ts, ts), 1)
    same_chunk = (r2 & -c) == (c2 & -c)
    causal_bd = same_chunk & (r2 >= c2)
    strict_bd = same_chunk & (r2 > c2)
    rp = lax.broadcasted_iota(jnp.int32, (c, ts), 0)
    cp = lax.broadcasted_iota(jnp.int32, (c, ts), 1) & (c - 1)
    eye_p = jnp.where(rp == cp, 1.0, 0.0)
    in_diag_block = (rp & -INV_BLOCK) == (cp & -INV_BLOCK)

    lane = lax.broadcasted_iota(jnp.int32, (c, hd), 1)
    lane_masks = [jnp.where((lane & -c) == p * c, 1.0, 0.0).astype(BF16) for p in range(hd // c)]
    zero_blk = jnp.zeros((c, hd), BF16)

    def blockdiag(y):
        yb = y.astype(BF16)
        rows = []
        for i in range(GDN_PACK):
            slab, off = divmod(i * c, hd)
            blocks = [zero_blk] * (ts // hd)
            blocks[slab] = yb[:, slab * hd:(slab + 1) * hd] * lane_masks[off // c]
            rows.append(jnp.concatenate(blocks, axis=1))
        return jnp.concatenate(rows, axis=0)

    def bmm(x, bd):
        return jnp.dot(x.astype(BF16), bd, preferred_element_type=F32)

    g_hi = g_all.astype(BF16)
    r_mid = g_all - g_hi.astype(F32)
    g_mid = r_mid.astype(BF16)
    g_lo = (r_mid - g_mid.astype(F32)).astype(BF16)
    sums = jnp.dot(jnp.where(causal_bd, 1.0, 0.0).astype(BF16), jnp.concatenate([g_hi, g_mid, g_lo], axis=1),
                   preferred_element_type=F32)
    gcum = sums[:, 0:LANES] + sums[:, LANES:2 * LANES] + sums[:, 2 * LANES:3 * LANES]
    glast = jnp.concatenate([jnp.broadcast_to(gcum[(ci + 1) * c - 1:(ci + 1) * c, :], (c, LANES))
                             for ci in range(GDN_PACK)], axis=0)
    gcum_t = gcum.T

    for g0 in range(0, heads, GDN_HEAD_GROUP):
        hg = range(g0, min(g0 + GDN_HEAD_GROUP, heads))
        prods = []
        for h in hg:
            qb = q_ref[:, h * hd:(h + 1) * hd]
            kb16 = k_ref[:, h * hd:(h + 1) * hd]
            q = qb.astype(F32)
            k = kb16.astype(F32)
            v = v_ref[:, h * hd:(h + 1) * hd].astype(F32)
            gcol = gcum[:, h:h + 1]
            egc = jnp.exp(gcol)
            beta = beta_all[:, heads + h:heads + h + 1]
            kb = k * beta
            prods.append(lax.dot_general(jnp.concatenate([qb, kb.astype(BF16)], axis=0), kb16,
                                         (((1,), (1,)), ((), ())), preferred_element_type=F32))
            rhs_s[h] = jnp.concatenate([v * beta, kb * egc], axis=1).astype(BF16)
            qd_s[h] = (q * egc).astype(BF16)
            kd_s[h] = (k * jnp.exp(glast[:, h:h + 1] - gcol)).astype(BF16)
        lps = []
        for n, h in enumerate(hg):
            gcol = gcum[:, h:h + 1]
            grow = gcum_t[h:h + 1, :]
            dmat = jnp.where(causal_bd, jnp.exp(jnp.where(causal_bd, gcol - grow, 0.0)), 0.0)
            qk_s[h] = (prods[n][0:ts] * dmat).astype(BF16)
            lbd = jnp.where(strict_bd, prods[n][ts:2 * ts] * dmat, 0.0)
            lp = lbd[0:c]
            for i in range(1, GDN_PACK):
                lp = lp + lbd[i * c:(i + 1) * c]
            lps.append(lp)

        ld = [jnp.where(in_diag_block, lp, 0.0) for lp in lps]
        tinv = [eye_p - d for d in ld]
        pw = [bmm(d, blockdiag(d)) for d in ld]
        span = 2
        while span < INV_BLOCK:
            last = 2 * span >= INV_BLOCK
            res = [bmm(d if last else jnp.concatenate([p, d], axis=0), blockdiag(p)) for p, d in zip(pw, tinv)]
            if last:
                tinv = [d + r for d, r in zip(tinv, res)]
            else:
                pw = [r[0:c] for r in res]
                tinv = [d + r[c:2 * c] for d, r in zip(tinv, res)]
            span *= 2
        b = INV_BLOCK
        while b < c:
            off = ((rp & -2 * b) == (cp & -2 * b)) & ((rp & -b) != (cp & -b))
            te = [bmm(x, blockdiag(jnp.where(off, lp, 0.0))) for x, lp in zip(tinv, lps)]
            tinv = [x - bmm(y, blockdiag(x)) for x, y in zip(tinv, te)]
            b *= 2

        for n, h in enumerate(hg):
            uw = jnp.dot(blockdiag(tinv[n]), rhs_s[h], preferred_element_type=F32)
            u_s[h] = uw[:, 0:hd]
            w_s[h] = uw[:, hd:2 * hd].astype(BF16)

    gain = gain_ref[...]
    zero_v = jnp.zeros((c, hd), BF16)
    for ci in range(GDN_PACK):
        r0 = ci * c
        slab = (r0 // hd) * hd
        wss = [_mm(jnp.concatenate([w_s[h, r0:r0 + c, :], qd_s[h, r0:r0 + c, :]], axis=0), state[h]) for h in hs]
        for h in hs:
            v_new = u_s[h, r0:r0 + c, :] - wss[h][0:c]
            v_new_b = v_new.astype(BF16)
            pieces = [zero_v] * (hd // c)
            pieces[(r0 - slab) // c] = v_new_b
            out = wss[h][c:2 * c] + jnp.dot(qk_s[h, r0:r0 + c, slab:slab + hd], jnp.concatenate(pieces, axis=0),
                                            preferred_element_type=F32)
            state[h] = (state[h] * jnp.exp(glast[r0:r0 + 1, h:h + 1])
                        + _mm_tn(kd_s[h, r0:r0 + c, :], v_new_b))
            o = out * lax.rsqrt(jnp.mean(out * out, axis=-1, keepdims=True) + RMS_EPS) * gain
            zz = z_ref[r0:r0 + c, h * hd:(h + 1) * hd].astype(F32)
            o_ref[r0:r0 + c, h * hd:(h + 1) * hd] = (o * _silu(zz)).astype(o_ref.dtype)


def _gdn(proj, ab, alog_row, dtb_row, gain_row, layer, batch, seq, heads, cols):
    width = heads * LANES
    ts = GDN_PACK * GDN_CHUNK
    nt = seq // ts
    kern = functools.partial(_gdn_kernel, heads=heads)
    return pl.pallas_call(
        kern,
        grid=(batch, nt),
        in_specs=[
            pl.BlockSpec((ts, width), lambda b, t: (b * nt + t, cols[0])),
            pl.BlockSpec((ts, width), lambda b, t: (b * nt + t, cols[1])),
            pl.BlockSpec((ts, width), lambda b, t: (b * nt + t, cols[2])),
            pl.BlockSpec((ts, width), lambda b, t: (b * nt + t, cols[3])),
            pl.BlockSpec((ts, LANES), lambda b, t: (b * nt + t, 0)),
            pl.BlockSpec((None, 1, LANES), lambda b, t: (layer, 0, 0)),
            pl.BlockSpec((None, 1, LANES), lambda b, t: (layer, 0, 0)),
            pl.BlockSpec((None, 1, LANES), lambda b, t: (layer, 0, 0)),
        ],
        out_specs=pl.BlockSpec((ts, width), lambda b, t: (b * nt + t, 0)),
        out_shape=jax.ShapeDtypeStruct((batch * seq, width), BF16),
        scratch_shapes=[
            pltpu.VMEM((heads, LANES, LANES), F32),
            pltpu.VMEM((heads, ts, ts), BF16),
            pltpu.VMEM((heads, ts, 2 * LANES), BF16),
            pltpu.VMEM((heads, ts, LANES), F32),
            pltpu.VMEM((heads, ts, LANES), BF16),
            pltpu.VMEM((heads, ts, LANES), BF16),
            pltpu.VMEM((heads, ts, LANES), BF16),
        ],
        compiler_params=pltpu.CompilerParams(
            dimension_semantics=("arbitrary", "arbitrary"), vmem_limit_bytes=VMEM_LIMIT),
        name="gdn",
    )(proj, proj, proj, proj, ab, alog_row, dtb_row, gain_row)


def _lru_kernel(xc_ref, yb_ref, wa_ref, ba_ref, wx_ref, bx_ref, lam_ref, o_ref,
                abuf, bbuf, hcar, *, blocks, ts):
    bd = LANES
    width = blocks * bd

    @pl.when(pl.program_id(1) == 0)
    def _():
        hcar[...] = jnp.zeros_like(hcar)

    for n in range(blocks):
        lo = n * bd
        xcb = xc_ref[:, lo:lo + bd]
        xc = xcb.astype(F32)
        r = _sigmoid(jnp.dot(xcb, wa_ref[n], preferred_element_type=F32) + ba_ref[0:1, lo:lo + bd])
        i = _sigmoid(jnp.dot(xcb, wx_ref[n], preferred_element_type=F32) + bx_ref[0:1, lo:lo + bd])
        log_a = (-LRU_C) * r * _softplus(-lam_ref[0:1, lo:lo + bd])
        a = jnp.exp(log_a)
        one_minus_a2 = -jnp.tanh(log_a) * (1.0 + a * a)
        abuf[:, lo:lo + bd] = a
        bbuf[:, lo:lo + bd] = jnp.sqrt(one_minus_a2) * (i * xc)

    rowi = lax.broadcasted_iota(jnp.int32, (SUBLANES, width), 0)

    def group(gi, hprev):
        r0 = pl.multiple_of(gi * SUBLANES, SUBLANES)
        a = abuf[pl.ds(r0, SUBLANES), :]
        b = bbuf[pl.ds(r0, SUBLANES), :]
        s = 1
        while s < SUBLANES:
            keep = rowi >= s
            a_sh = jnp.where(keep, pltpu.roll(a, s, axis=0), 1.0)
            b_sh = jnp.where(keep, pltpu.roll(b, s, axis=0), 0.0)
            b = a * b_sh + b
            a = a * a_sh
            s *= 2
        bbuf[pl.ds(r0, SUBLANES), :] = a * hprev + b
        return a[SUBLANES - 1:SUBLANES, :] * hprev + b[SUBLANES - 1:SUBLANES, :]

    hcar[...] = lax.fori_loop(0, ts // SUBLANES, group, hcar[...], unroll=4)
    o_ref[...] = (bbuf[...] * _gelu_tanh(yb_ref[...].astype(F32))).astype(o_ref.dtype)


def _lru(proj, w_a, b_a, w_x, b_x, lam, layer, batch, seq, blocks, xb_col, yb_col, ts):
    width = blocks * LANES
    nt = seq // ts
    kern = functools.partial(_lru_kernel, blocks=blocks, ts=ts)
    row = lambda b, t: (layer, 0, 0)
    mat = lambda b, t: (layer, 0, 0, 0)
    return pl.pallas_call(
        kern,
        grid=(batch, nt),
        in_specs=[
            pl.BlockSpec((ts, width), lambda b, t: (b * nt + t, xb_col)),
            pl.BlockSpec((ts, width), lambda b, t: (b * nt + t, yb_col)),
            pl.BlockSpec((None, blocks, LANES, LANES), mat),
            pl.BlockSpec((None, 1, width), row),
            pl.BlockSpec((None, blocks, LANES, LANES), mat),
            pl.BlockSpec((None, 1, width), row),
            pl.BlockSpec((None, 1, width), row),
        ],
        out_specs=pl.BlockSpec((ts, width), lambda b, t: (b * nt + t, 0)),
        out_shape=jax.ShapeDtypeStruct((batch * seq, width), BF16),
        scratch_shapes=[
            pltpu.VMEM((ts, width), F32),
            pltpu.VMEM((ts, width), F32),
            pltpu.VMEM((1, width), F32),
        ],
        compiler_params=pltpu.CompilerParams(
            dimension_semantics=("arbitrary", "arbitrary"), vmem_limit_bytes=VMEM_LIMIT),
        name="lru",
    )(proj, proj, w_a, b_a, w_x, b_x, lam)


def _merge_kernel(x_ref, og_ref, ol_ref, gg0_ref, gg1_ref, gl_ref, wg_ref, wl_ref, wo_ref, c0_ref, c1_ref,
                  o_ref, d0_ref, d1_ref):
    _cast_step((c0_ref, c1_ref), (d0_ref, d1_ref))
    pg = jnp.dot(og_ref[...], wg_ref[...], preferred_element_type=F32)
    pb = jnp.dot(ol_ref[...], wl_ref[...], preferred_element_type=F32)
    gg = jnp.concatenate([gg0_ref[...], gg1_ref[...]], axis=1)
    merged = (_sigmoid(gg.astype(F32)) * pg
              + _sigmoid(gl_ref[...].astype(F32)) * pb)
    o_ref[...] = x_ref[...] + jnp.dot(merged.astype(BF16), wo_ref[...], preferred_element_type=F32)


def _merge(x2, o_gdn, o_lru, proj, w_bg, w_bl, w_out, casts, layer, gg_cols, gl_col, tm):
    m, d = x2.shape
    wg = o_gdn.shape[1]
    wl = o_lru.shape[1]
    const = lambda i: (0, 0)
    steps = m // tm
    cast = [_cast_specs(w, layer, steps, lambda i: i) for w in casts]
    return pl.pallas_call(
        _merge_kernel,
        grid=(steps,),
        in_specs=[
            pl.BlockSpec((tm, d), lambda i: (i, 0)),
            pl.BlockSpec((tm, wg), lambda i: (i, 0)),
            pl.BlockSpec((tm, wl), lambda i: (i, 0)),
            pl.BlockSpec((tm, d // 2), lambda i: (i, gg_cols[0])),
            pl.BlockSpec((tm, d // 2), lambda i: (i, gg_cols[1])),
            pl.BlockSpec((tm, d), lambda i: (i, gl_col)),
            pl.BlockSpec((wg, d), const, pipeline_mode=pl.Buffered(1)),
            pl.BlockSpec((wl, d), const, pipeline_mode=pl.Buffered(1)),
            pl.BlockSpec((d, d), const, pipeline_mode=pl.Buffered(1)),
        ] + [c[0] for c in cast],
        out_specs=[pl.BlockSpec((tm, d), lambda i: (i, 0))] + [c[1] for c in cast],
        out_shape=[jax.ShapeDtypeStruct((m, d), F32)] + [c[2] for c in cast],
        compiler_params=pltpu.CompilerParams(
            dimension_semantics=("arbitrary",), vmem_limit_bytes=VMEM_LIMIT),
        name="merge",
    )(x2, o_gdn, o_lru, proj, proj, proj, w_bg, w_bl, w_out, *casts)


def _mlp_kernel(x_ref, g_ref, wu_ref, wd_ref, fg_ref, o_ref, h_ref, *, final_norm):
    f = pl.program_id(1)

    @pl.when(f == 0)
    def _():
        x = x_ref[...]
        ms = jnp.mean(x * x, axis=-1, keepdims=True)
        h_ref[...] = (x * lax.rsqrt(ms + RMS_EPS) * g_ref[...]).astype(BF16)
        o_ref[...] = x

    u = jnp.dot(h_ref[...], wu_ref[...], preferred_element_type=F32)
    u = jnp.maximum(u, 0.0)
    o_ref[...] += jnp.dot((u * u).astype(BF16), wd_ref[...], preferred_element_type=F32)

    if final_norm:
        @pl.when(f == pl.num_programs(1) - 1)
        def _():
            y = o_ref[...]
            ms = jnp.mean(y * y, axis=-1, keepdims=True)
            o_ref[...] = y * lax.rsqrt(ms + RMS_EPS) * fg_ref[...]


def _mlp(x2, gain, w_up, w_down, final_gain, layer, final_norm, tm, tf):
    m, d = x2.shape
    ff = w_up.shape[1]
    kern = functools.partial(_mlp_kernel, final_norm=final_norm)
    return pl.pallas_call(
        kern,
        grid=(m // tm, ff // tf),
        in_specs=[
            pl.BlockSpec((tm, d), lambda i, f: (i, 0)),
            pl.BlockSpec((None, 1, d), lambda i, f: (layer, 0, 0)),
            pl.BlockSpec((d, tf), lambda i, f: (0, f)),
            pl.BlockSpec((tf, d), lambda i, f: (f, 0)),
            pl.BlockSpec((1, d), lambda i, f: (0, 0)),
        ],
        out_specs=pl.BlockSpec((tm, d), lambda i, f: (i, 0)),
        out_shape=jax.ShapeDtypeStruct((m, d), F32),
        scratch_shapes=[pltpu.VMEM((tm, d), BF16)],
        compiler_params=pltpu.CompilerParams(
            dimension_semantics=("arbitrary", "arbitrary"), vmem_limit_bytes=VMEM_LIMIT),
        name="mlp",
    )(x2, gain, w_up, w_down, final_gain)


def _tile(n, want):
    t = min(want, n)
    assert n % t == 0, (n, want)
    return t


def kernel(x, attn_norm, w_in, gdn_conv_w, gdn_a_log, gdn_dt_bias, gdn_norm, lru_conv_w, lru_conv_b,
           lru_w_a, lru_b_a, lru_w_x, lru_b_x, lru_lambda, w_branch_gdn, w_branch_lru, w_out,
           mlp_norm, w_up, w_down, final_norm):
    batch, seq, d = x.shape
    depth = w_in.shape[0]
    heads = gdn_a_log.shape[1]
    gw = heads * LANES
    assert gdn_norm.shape[1] == LANES and gdn_conv_w.shape[2] == 3 * gw
    blocks = lru_w_a.shape[1]
    lw = blocks * LANES
    assert lru_w_a.shape[2] == LANES
    assert 2 * heads <= LANES and seq % (GDN_PACK * GDN_CHUNK) == 0
    assert gw == lw
    m = batch * seq

    o_z = 3 * gw
    o_a = o_z + gw
    o_xb = o_a + 2 * heads
    o_yb = o_xb + lw
    o_gg = o_yb + lw
    o_gl = o_gg + d
    assert w_in.shape[2] == o_gl + d
    assert d == 2 * gw
    a_rows = (0, gw, 2 * gw, o_xb, o_gl)
    b_rows = (o_z, o_yb, o_gg, o_gg + gw, o_gl + gw)
    assert all(r % BF16_ROWS == 0 for r in a_rows + b_rows + (o_a,))
    gdn_cols = (0, 2, 4, 1)
    yb_col, xb_col = 3, 6
    gg_cols, gl_col = (5, 7), 4
    lru_step = 3

    tm_in = _tile(seq, 1024)
    ts_lru = _tile(seq, 512)
    tm_merge = _tile(m, 256)
    tm_mlp = _tile(m, 1024)
    tf_mlp = _tile(w_up.shape[2], 1024)

    w_t = jnp.swapaxes(w_in, 1, 2).astype(BF16)
    zeros_w = jnp.zeros((depth, CONV_WIDTH, gw), F32)
    conv_w = jnp.concatenate([gdn_conv_w.astype(F32), lru_conv_w.astype(F32), zeros_w], axis=2)
    conv_b = jnp.pad(lru_conv_b.astype(F32), ((0, 0), (3 * gw, gw)))[:, None, :]
    pad_h = ((0, 0), (0, LANES - heads))
    alog_rows = jnp.pad(gdn_a_log.astype(F32), pad_h)[:, None, :]
    dtb_rows = jnp.pad(gdn_dt_bias.astype(F32), pad_h)[:, None, :]
    w_a_b, w_x_b = lru_w_a.astype(BF16), lru_w_x.astype(BF16)
    row3 = lambda v: v[:, None, :]

    x2 = x.reshape(m, d)
    for l in range(depth):
        proj, ab, w_bg_b, w_bl_b, w_out_b = _in_proj(
            x2, row3(attn_norm), w_t, conv_w, conv_b, (w_branch_gdn, w_branch_lru, w_out),
            l, tm_in, gw, seq, lru_step, a_rows, b_rows, o_a)
        o_gdn = _gdn(proj, ab, alog_rows, dtb_rows, row3(gdn_norm), l, batch, seq, heads, gdn_cols)
        o_lru = _lru(proj, w_a_b, row3(lru_b_a), w_x_b, row3(lru_b_x),
                     row3(lru_lambda), l, batch, seq, blocks, xb_col, yb_col, ts_lru)
        x2, w_up_b, w_down_b = _merge(x2, o_gdn, o_lru, proj, w_bg_b, w_bl_b, w_out_b, (w_up, w_down),
                                      l, gg_cols, gl_col, tm_merge)
        x2 = _mlp(x2, row3(mlp_norm), w_up_b, w_down_b, final_norm.reshape(1, d), l, l == depth - 1,
                  tm_mlp, tf_mlp)
    if depth == 0:
        raise ValueError("depth must be positive")
    return x2.reshape(batch, seq, d)
```

```python
import functools

import jax
import jax.numpy as jnp
from jax import lax
from jax.experimental import pallas as pl
from jax.experimental.pallas import tpu as pltpu

F32 = jnp.float32
BF16 = jnp.bfloat16

RMS_EPS = 1e-6
L2_EPS = 1e-6
LRU_C = 8.0
CONV_WIDTH = 4
GDN_CHUNK = 64
INV_BLOCK = 8
GDN_HEAD_GROUP = 8
GDN_PACK = 4
GDN_TILES = 2
CONV_COLS = 256
LANES = 128
SUBLANES = 8
BF16_ROWS = 16
VMEM_LIMIT = 60 * 1024 * 1024


def _mm(a, b):
    return jnp.dot(a.astype(BF16), b.astype(BF16), preferred_element_type=F32)


def _mm_tn(a, b):
    return lax.dot_general(a.astype(BF16), b.astype(BF16), (((0,), (0,)), ((), ())),
                           preferred_element_type=F32)


def _softplus(x):
    return jnp.maximum(x, 0.0) + jnp.log1p(jnp.exp(-jnp.abs(x)))


def _silu(x):
    h = 0.5 * x
    return h + h * jnp.tanh(h)


def _gelu_tanh(x):
    c = 0.7978845608028654
    return 0.5 * x * (1.0 + jnp.tanh(c * (x + 0.044715 * (x * x * x))))


def _cast_plan(w, steps):
    rows = w.shape[1]
    units = rows // BF16_ROWS
    assert rows % BF16_ROWS == 0
    nb = max(k for k in range(1, min(units, steps) + 1) if units % k == 0)
    return nb, rows // nb


def _cast_specs(w, layer, steps, step_of):
    nb, rb = _cast_plan(w, steps)
    cols = w.shape[2]
    blk = lambda *ids: jnp.minimum(step_of(*ids), nb - 1)
    return (pl.BlockSpec((None, rb, cols), lambda *ids: (layer, blk(*ids), 0)),
            pl.BlockSpec((rb, cols), lambda *ids: (blk(*ids), 0)),
            jax.ShapeDtypeStruct((w.shape[1], cols), BF16))


def _cast_step(srcs, dsts):
    for src, dst in zip(srcs, dsts):
        dst[...] = src[...].astype(dst.dtype)


def _dot_nt(a, b):
    return lax.dot_general(a, b, (((1,), (1,)), ((), ())), preferred_element_type=F32)


def _in_proj_kernel(x_ref, g_ref, wa_ref, wb_ref, wab_ref, cw_ref, cb_ref, c0_ref, c1_ref, c2_ref,
                    o_ref, oab_ref, d0_ref, d1_ref, d2_ref, h_ref, tail_ref, *,
                    tiles_per_seq, lru_step):
    i = pl.program_id(0)
    j = pl.program_id(1)
    tm, tn = o_ref.shape
    half = tn // 2

    def ride_casts():
        _cast_step((c0_ref, c1_ref, c2_ref), (d0_ref, d1_ref, d2_ref))

    @pl.when(j == 0)
    def _():
        x = x_ref[...]
        ms = jnp.mean(x * x, axis=-1, keepdims=True)
        h = (x * lax.rsqrt(ms + RMS_EPS) * g_ref[...]).astype(BF16)
        h_ref[...] = h
        oab_ref[...] = _dot_nt(h, wab_ref[...])

        @pl.when(i == 0)
        def _():
            tail_ref[...] = jnp.zeros_like(tail_ref)

    cc = min(half, CONV_COLS)
    row8 = lax.broadcasted_iota(jnp.int32, (SUBLANES, cc), 0)

    def shift_rows(val, prev8, s):
        rolled = pltpu.roll(val, s, axis=0)
        top = jnp.where(row8 < s, pltpu.roll(prev8, s, axis=0), rolled[0:SUBLANES, :])
        return jnp.concatenate([top, rolled[SUBLANES:, :]], axis=0)

    def conv_chunks(finish):
        ride_casts()
        first = i % tiles_per_seq == 0
        nb = half // cc
        for ci in range(nb):
            c0 = ci * cc
            col = pl.multiple_of(j * half, half) + c0
            prev = jnp.where(first, 0.0, tail_ref[:, pl.ds(col, cc)])
            res = _dot_nt(h_ref[...], wa_ref[c0:c0 + cc, :])
            tail_ref[:, pl.ds(col, cc)] = res[tm - SUBLANES:tm, :]
            b0 = half + c0
            o_ref[:, b0:b0 + cc] = _dot_nt(h_ref[...], wb_ref[c0:c0 + cc, :]).astype(o_ref.dtype)
            taps = [cw_ref[k:k + 1, c0:c0 + cc] for k in range(CONV_WIDTH)]
            s1 = shift_rows(res, prev, 1)
            u = res * taps[3] + s1 * taps[2]
            v = res * taps[1] + s1 * taps[0]
            prev_v = prev * taps[1] + pltpu.roll(prev, 1, axis=0) * taps[0]
            finish(u + shift_rows(v, prev_v, 2), c0)

    @pl.when(j < 3)
    def _():
        is_v = j == 2
        q_scale = jnp.where(j == 0, LANES ** -0.5, 1.0)

        def finish(acc, c0):
            y = _silu(acc)
            for g in range(cc // LANES):
                yg = y[:, g * LANES:(g + 1) * LANES]
                inv = lax.rsqrt(jnp.sum(yg * yg, axis=-1, keepdims=True) + L2_EPS)
                o_ref[:, c0 + g * LANES:c0 + (g + 1) * LANES] = (
                    yg * jnp.where(is_v, 1.0, inv * q_scale)).astype(o_ref.dtype)

        conv_chunks(finish)

    @pl.when(j == lru_step)
    def _():
        def finish(acc, c0):
            o_ref[:, c0:c0 + cc] = (acc + cb_ref[:, c0:c0 + cc]).astype(o_ref.dtype)

        conv_chunks(finish)

    @pl.when((j >= 3) & (j != lru_step))
    def _():
        ride_casts()
        o_ref[:, 0:half] = _dot_nt(h_ref[...], wa_ref[...]).astype(o_ref.dtype)
        o_ref[:, half:tn] = _dot_nt(h_ref[...], wb_ref[...]).astype(o_ref.dtype)


def _in_proj(x2, gain, w_t, conv_w, conv_b, casts, layer, tm, half, seq, lru_step, a_rows, b_rows, ab_row):
    m, d = x2.shape
    nj = len(a_rows)
    tn = 2 * half
    n = nj * tn
    steps = (m // tm) * nj
    cast = [_cast_specs(w, layer, steps, lambda i, j: i * nj + j) for w in casts]
    kern = functools.partial(_in_proj_kernel, tiles_per_seq=seq // tm, lru_step=lru_step)

    def rows_of(table):
        return lambda i, j: (layer, BF16_ROWS * sum(jnp.where(j == s, r // BF16_ROWS, 0)
                                                     for s, r in enumerate(table)), 0)

    w_tile = (None, pl.Element(half), pl.Element(d))
    return pl.pallas_call(
        kern,
        grid=(m // tm, nj),
        in_specs=[
            pl.BlockSpec((tm, d), lambda i, j: (i, 0)),
            pl.BlockSpec((None, 1, d), lambda i, j: (layer, 0, 0)),
            pl.BlockSpec(w_tile, rows_of(a_rows)),
            pl.BlockSpec(w_tile, rows_of(b_rows)),
            pl.BlockSpec((None, pl.Element(LANES), pl.Element(d)), lambda i, j: (layer, ab_row, 0)),
            pl.BlockSpec((None, CONV_WIDTH, half), lambda i, j: (layer, 0, j)),
            pl.BlockSpec((None, 1, half), lambda i, j: (layer, 0, j)),
        ] + [c[0] for c in cast],
        out_specs=[
            pl.BlockSpec((tm, tn), lambda i, j: (i, j)),
            pl.BlockSpec((tm, LANES), lambda i, j: (i, 0)),
        ] + [c[1] for c in cast],
        out_shape=[
            jax.ShapeDtypeStruct((m, n), BF16),
            jax.ShapeDtypeStruct((m, LANES), F32),
        ] + [c[2] for c in cast],
        scratch_shapes=[pltpu.VMEM((tm, d), BF16), pltpu.VMEM((SUBLANES, n // 2), F32)],
        compiler_params=pltpu.CompilerParams(
            dimension_semantics=("arbitrary", "arbitrary"), vmem_limit_bytes=VMEM_LIMIT),
        name="in_proj",
    )(x2, gain, w_t, w_t, w_t, conv_w, conv_b, *casts)


def _gdn_kernel(q_ref, k_ref, v_ref, z_ref, ab_ref, alog_ref, dtb_ref, gain_ref, o_ref,
                state, qk_s, rhs_s, u_s, w_s, qd_s, kd_s, *, heads):
    ts = GDN_PACK * GDN_CHUNK

    @pl.when(pl.program_id(1) == 0)
    def _():
        state[...] = jnp.zeros_like(state)

    for tile in range(q_ref.shape[0] // ts):
        rows = pl.ds(tile * ts, ts)
        _gdn_tile(q_ref.at[rows], k_ref.at[rows], v_ref.at[rows], z_ref.at[rows], ab_ref.at[rows],
                  alog_ref, dtb_ref, gain_ref, o_ref.at[rows], state, qk_s, rhs_s, u_s, w_s, qd_s, kd_s, heads)


def _gdn_tile(q_ref, k_ref, v_ref, z_ref, ab_ref, alog_ref, dtb_ref, gain_ref, o_ref,
              state, qk_s, rhs_s, u_s, w_s, qd_s, kd_s, heads):
    hd = LANES
    width = heads * hd
    c = GDN_CHUNK
    ts = GDN_PACK * c
    hs = range(heads)

    ab = ab_ref[...]
    g_all = -jnp.exp(alog_ref[...]) * _softplus(ab + dtb_ref[...])
    beta_all = jax.nn.sigmoid(ab)

    r2 = lax.broadcasted_iota(jnp.int32, (ts, ts), 0)
    c2 = lax.broadcasted_iota(jnp.int32, (ts, ts), 1)
    same_chunk = (r2 & -c) == (c2 & -c)
    causal_bd = same_chunk & (r2 >= c2)
    strict_bd = same_chunk & (r2 > c2)
    rp = lax.broadcasted_iota(jnp.int32, (c, ts), 0)
    cp = lax.broadcasted_iota(jnp.int32, (c, ts), 1) & (c - 1)
    eye_p = jnp.where(rp == cp, 1.0, 0.0)
    in_diag_block = (rp & -INV_BLOCK) == (cp & -INV_BLOCK)

    lane = lax.broadcasted_iota(jnp.int32, (c, hd), 1)
    lane_masks = [jnp.where((lane & -c) == p * c, 1.0, 0.0).astype(BF16) for p in range(hd // c)]
    zero_blk = jnp.zeros((c, hd), BF16)

    def blockdiag(y):
        yb = y.astype(BF16)
        rows = []
        for i in range(GDN_PACK):
            slab, off = divmod(i * c, hd)
            blocks = [zero_blk] * (ts // hd)
            blocks[slab] = yb[:, slab * hd:(slab + 1) * hd] * lane_masks[off // c]
            rows.append(jnp.concatenate(blocks, axis=1))
        return jnp.concatenate(rows, axis=0)

    def bmm(x, bd):
        return jnp.dot(x.astype(BF16), bd, preferred_element_type=F32)

    g_hi = g_all.astype(BF16)
    r_mid = g_all - g_hi.astype(F32)
    g_mid = r_mid.astype(BF16)
    g_lo = (r_mid - g_mid.astype(F32)).astype(BF16)
    sums = jnp.dot(jnp.where(causal_bd, 1.0, 0.0).astype(BF16), jnp.concatenate([g_hi, g_mid, g_lo], axis=1),
                   preferred_element_type=F32)
    gcum = sums[:, 0:LANES] + sums[:, LANES:2 * LANES] + sums[:, 2 * LANES:3 * LANES]
    glast = jnp.concatenate([jnp.broadcast_to(gcum[(ci + 1) * c - 1:(ci + 1) * c, :], (c, LANES))
                             for ci in range(GDN_PACK)], axis=0)
    gcum_t = gcum.T

    for g0 in range(0, heads, GDN_HEAD_GROUP):
        hg = range(g0, min(g0 + GDN_HEAD_GROUP, heads))
        prods = []
        for h in hg:
            qb = q_ref[:, h * hd:(h + 1) * hd]
            kb16 = k_ref[:, h * hd:(h + 1) * hd]
            q = qb.astype(F32)
            k = kb16.astype(F32)
            v = v_ref[:, h * hd:(h + 1) * hd].astype(F32)
            gcol = gcum[:, h:h + 1]
            egc = jnp.exp(gcol)
            beta = beta_all[:, heads + h:heads + h + 1]
            kb = k * beta
            prods.append(lax.dot_general(jnp.concatenate([qb, kb.astype(BF16)], axis=0), kb16,
                                         (((1,), (1,)), ((), ())), preferred_element_type=F32))
            rhs_s[h] = jnp.concatenate([v * beta, kb * egc], axis=1).astype(BF16)
            qd_s[h] = (q * egc).astype(BF16)
            kd_s[h] = (k * jnp.exp(glast[:, h:h + 1] - gcol)).astype(BF16)
        lps = []
        for n, h in enumerate(hg):
            gcol = gcum[:, h:h + 1]
            grow = gcum_t[h:h + 1, :]
            dmat = jnp.where(causal_bd, jnp.exp(jnp.where(causal_bd, gcol - grow, 0.0)), 0.0)
            qk_s[h] = (prods[n][0:ts] * dmat).astype(BF16)
            lbd = jnp.where(strict_bd, prods[n][ts:2 * ts] * dmat, 0.0)
            lp = lbd[0:c]
            for i in range(1, GDN_PACK):
                lp = lp + lbd[i * c:(i + 1) * c]
            lps.append(lp)

        ld = [jnp.where(in_diag_block, lp, 0.0) for lp in lps]
        tinv = [eye_p - d for d in ld]
        pw = [bmm(d, blockdiag(d)) for d in ld]
        span = 2
        while span < INV_BLOCK:
            last = 2 * span >= INV_BLOCK
            res = [bmm(d if last else jnp.concatenate([p, d], axis=0), blockdiag(p)) for p, d in zip(pw, tinv)]
            if last:
                tinv = [d + r for d, r in zip(tinv, res)]
            else:
                pw = [r[0:c] for r in res]
                tinv = [d + r[c:2 * c] for d, r in zip(tinv, res)]
            span *= 2
        b = INV_BLOCK
        while b < c:
            off = ((rp & -2 * b) == (cp & -2 * b)) & ((rp & -b) != (cp & -b))
            te = [bmm(x, blockdiag(jnp.where(off, lp, 0.0))) for x, lp in zip(tinv, lps)]
            tinv = [x - bmm(y, blockdiag(x)) for x, y in zip(tinv, te)]
            b *= 2

        for n, h in enumerate(hg):
            uw = jnp.dot(blockdiag(tinv[n]), rhs_s[h], preferred_element_type=F32)
            u_s[h] = uw[:, 0:hd]
            w_s[h] = uw[:, hd:2 * hd].astype(BF16)

    gain = gain_ref[...]
    zero_v = jnp.zeros((c, hd), BF16)
    for ci in range(GDN_PACK):
        r0 = ci * c
        slab = (r0 // hd) * hd
        wss = [_mm(jnp.concatenate([w_s[h, r0:r0 + c, :], qd_s[h, r0:r0 + c, :]], axis=0), state[h]) for h in hs]
        for h in hs:
            v_new = u_s[h, r0:r0 + c, :] - wss[h][0:c]
            v_new_b = v_new.astype(BF16)
            pieces = [zero_v] * (hd // c)
            pieces[(r0 - slab) // c] = v_new_b
            out = wss[h][c:2 * c] + jnp.dot(qk_s[h, r0:r0 + c, slab:slab + hd], jnp.concatenate(pieces, axis=0),
                                            preferred_element_type=F32)
            state[h] = (state[h] * jnp.exp(glast[r0:r0 + 1, h:h + 1])
                        + _mm_tn(kd_s[h, r0:r0 + c, :], v_new_b))
            o = out * lax.rsqrt(jnp.mean(out * out, axis=-1, keepdims=True) + RMS_EPS) * gain
            zz = z_ref[r0:r0 + c, h * hd:(h + 1) * hd].astype(F32)
            o_ref[r0:r0 + c, h * hd:(h + 1) * hd] = (o * _silu(zz)).astype(o_ref.dtype)


def _gdn(proj, ab, alog_row, dtb_row, gain_row, layer, batch, seq, heads, cols):
    width = heads * LANES
    tt = GDN_PACK * GDN_CHUNK
    ts = GDN_TILES * tt
    nt = seq // ts
    kern = functools.partial(_gdn_kernel, heads=heads)
    return pl.pallas_call(
        kern,
        grid=(batch, nt),
        in_specs=[
            pl.BlockSpec((ts, width), lambda b, t: (b * nt + t, cols[0])),
            pl.BlockSpec((ts, width), lambda b, t: (b * nt + t, cols[1])),
            pl.BlockSpec((ts, width), lambda b, t: (b * nt + t, cols[2])),
            pl.BlockSpec((ts, width), lambda b, t: (b * nt + t, cols[3])),
            pl.BlockSpec((ts, LANES), lambda b, t: (b * nt + t, 0)),
            pl.BlockSpec((None, 1, LANES), lambda b, t: (layer, 0, 0)),
            pl.BlockSpec((None, 1, LANES), lambda b, t: (layer, 0, 0)),
            pl.BlockSpec((None, 1, LANES), lambda b, t: (layer, 0, 0)),
        ],
        out_specs=pl.BlockSpec((ts, width), lambda b, t: (b * nt + t, 0)),
        out_shape=jax.ShapeDtypeStruct((batch * seq, width), BF16),
        scratch_shapes=[
            pltpu.VMEM((heads, LANES, LANES), F32),
            pltpu.VMEM((heads, tt, tt), BF16),
            pltpu.VMEM((heads, tt, 2 * LANES), BF16),
            pltpu.VMEM((heads, tt, LANES), F32),
            pltpu.VMEM((heads, tt, LANES), BF16),
            pltpu.VMEM((heads, tt, LANES), BF16),
            pltpu.VMEM((heads, tt, LANES), BF16),
        ],
        compiler_params=pltpu.CompilerParams(
            dimension_semantics=("arbitrary", "arbitrary"), vmem_limit_bytes=VMEM_LIMIT),
        name="gdn",
    )(proj, proj, proj, proj, ab, alog_row, dtb_row, gain_row)


def _lru_kernel(xc_ref, yb_ref, wa_ref, ba_ref, wx_ref, bx_ref, lam_ref, o_ref,
                abuf, bbuf, hcar, *, blocks, ts):
    bd = LANES
    width = blocks * bd

    @pl.when(pl.program_id(1) == 0)
    def _():
        hcar[...] = jnp.zeros_like(hcar)

    for n in range(blocks):
        lo = n * bd
        xcb = xc_ref[:, lo:lo + bd]
        xc = xcb.astype(F32)
        r = jax.nn.sigmoid(jnp.dot(xcb, wa_ref[n], preferred_element_type=F32) + ba_ref[0:1, lo:lo + bd])
        i = jax.nn.sigmoid(jnp.dot(xcb, wx_ref[n], preferred_element_type=F32) + bx_ref[0:1, lo:lo + bd])
        log_a = (-LRU_C) * r * _softplus(-lam_ref[0:1, lo:lo + bd])
        a = jnp.exp(log_a)
        one_minus_a2 = -jnp.tanh(log_a) * (1.0 + a * a)
        abuf[:, lo:lo + bd] = a
        bbuf[:, lo:lo + bd] = jnp.sqrt(one_minus_a2) * (i * xc)

    rowi = lax.broadcasted_iota(jnp.int32, (SUBLANES, width), 0)

    def group(gi, hprev):
        r0 = pl.multiple_of(gi * SUBLANES, SUBLANES)
        a = abuf[pl.ds(r0, SUBLANES), :]
        b = bbuf[pl.ds(r0, SUBLANES), :]
        s = 1
        while s < SUBLANES:
            keep = rowi >= s
            a_sh = jnp.where(keep, pltpu.roll(a, s, axis=0), 1.0)
            b_sh = jnp.where(keep, pltpu.roll(b, s, axis=0), 0.0)
            b = a * b_sh + b
            a = a * a_sh
            s *= 2
        bbuf[pl.ds(r0, SUBLANES), :] = a * hprev + b
        return a[SUBLANES - 1:SUBLANES, :] * hprev + b[SUBLANES - 1:SUBLANES, :]

    hcar[...] = lax.fori_loop(0, ts // SUBLANES, group, hcar[...], unroll=4)
    o_ref[...] = (bbuf[...] * _gelu_tanh(yb_ref[...].astype(F32))).astype(o_ref.dtype)


def _lru(proj, w_a, b_a, w_x, b_x, lam, layer, batch, seq, blocks, xb_col, yb_col, ts):
    width = blocks * LANES
    nt = seq // ts
    kern = functools.partial(_lru_kernel, blocks=blocks, ts=ts)
    row = lambda b, t: (layer, 0, 0)
    mat = lambda b, t: (layer, 0, 0, 0)
    return pl.pallas_call(
        kern,
        grid=(batch, nt),
        in_specs=[
            pl.BlockSpec((ts, width), lambda b, t: (b * nt + t, xb_col)),
            pl.BlockSpec((ts, width), lambda b, t: (b * nt + t, yb_col)),
            pl.BlockSpec((None, blocks, LANES, LANES), mat),
            pl.BlockSpec((None, 1, width), row),
            pl.BlockSpec((None, blocks, LANES, LANES), mat),
            pl.BlockSpec((None, 1, width), row),
            pl.BlockSpec((None, 1, width), row),
        ],
        out_specs=pl.BlockSpec((ts, width), lambda b, t: (b * nt + t, 0)),
        out_shape=jax.ShapeDtypeStruct((batch * seq, width), BF16),
        scratch_shapes=[
            pltpu.VMEM((ts, width), F32),
            pltpu.VMEM((ts, width), F32),
            pltpu.VMEM((1, width), F32),
        ],
        compiler_params=pltpu.CompilerParams(
            dimension_semantics=("arbitrary", "arbitrary"), vmem_limit_bytes=VMEM_LIMIT),
        name="lru",
    )(proj, proj, w_a, b_a, w_x, b_x, lam)


def _merge_kernel(x_ref, og_ref, ol_ref, gg0_ref, gg1_ref, gl_ref, wg_ref, wl_ref, wo_ref, c0_ref, c1_ref,
                  o_ref, d0_ref, d1_ref):
    _cast_step((c0_ref, c1_ref), (d0_ref, d1_ref))
    pg = jnp.dot(og_ref[...], wg_ref[...], preferred_element_type=F32)
    pb = jnp.dot(ol_ref[...], wl_ref[...], preferred_element_type=F32)
    gg = jnp.concatenate([gg0_ref[...], gg1_ref[...]], axis=1)
    merged = (jax.nn.sigmoid(gg.astype(F32)) * pg
              + jax.nn.sigmoid(gl_ref[...].astype(F32)) * pb)
    o_ref[...] = x_ref[...] + jnp.dot(merged.astype(BF16), wo_ref[...], preferred_element_type=F32)


def _merge(x2, o_gdn, o_lru, proj, w_bg, w_bl, w_out, casts, layer, gg_cols, gl_col, tm):
    m, d = x2.shape
    wg = o_gdn.shape[1]
    wl = o_lru.shape[1]
    const = lambda i: (0, 0)
    steps = m // tm
    cast = [_cast_specs(w, layer, steps, lambda i: i) for w in casts]
    return pl.pallas_call(
        _merge_kernel,
        grid=(steps,),
        in_specs=[
            pl.BlockSpec((tm, d), lambda i: (i, 0)),
            pl.BlockSpec((tm, wg), lambda i: (i, 0)),
            pl.BlockSpec((tm, wl), lambda i: (i, 0)),
            pl.BlockSpec((tm, d // 2), lambda i: (i, gg_cols[0])),
            pl.BlockSpec((tm, d // 2), lambda i: (i, gg_cols[1])),
            pl.BlockSpec((tm, d), lambda i: (i, gl_col)),
            pl.BlockSpec((wg, d), const, pipeline_mode=pl.Buffered(1)),
            pl.BlockSpec((wl, d), const, pipeline_mode=pl.Buffered(1)),
            pl.BlockSpec((d, d), const, pipeline_mode=pl.Buffered(1)),
        ] + [c[0] for c in cast],
        out_specs=[pl.BlockSpec((tm, d), lambda i: (i, 0))] + [c[1] for c in cast],
        out_shape=[jax.ShapeDtypeStruct((m, d), F32)] + [c[2] for c in cast],
        compiler_params=pltpu.CompilerParams(
            dimension_semantics=("arbitrary",), vmem_limit_bytes=VMEM_LIMIT),
        name="merge",
    )(x2, o_gdn, o_lru, proj, proj, proj, w_bg, w_bl, w_out, *casts)


def _mlp_kernel(x_ref, g_ref, wu_ref, wd_ref, fg_ref, o_ref, h_ref, *, final_norm):
    f = pl.program_id(1)

    @pl.when(f == 0)
    def _():
        x = x_ref[...]
        ms = jnp.mean(x * x, axis=-1, keepdims=True)
        h_ref[...] = (x * lax.rsqrt(ms + RMS_EPS) * g_ref[...]).astype(BF16)
        o_ref[...] = x

    u = jnp.dot(h_ref[...], wu_ref[...], preferred_element_type=F32)
    u = jnp.maximum(u, 0.0)
    o_ref[...] += jnp.dot((u * u).astype(BF16), wd_ref[...], preferred_element_type=F32)

    if final_norm:
        @pl.when(f == pl.num_programs(1) - 1)
        def _():
            y = o_ref[...]
            ms = jnp.mean(y * y, axis=-1, keepdims=True)
            o_ref[...] = y * lax.rsqrt(ms + RMS_EPS) * fg_ref[...]


def _mlp(x2, gain, w_up, w_down, final_gain, layer, final_norm, tm, tf):
    m, d = x2.shape
    ff = w_up.shape[1]
    kern = functools.partial(_mlp_kernel, final_norm=final_norm)
    return pl.pallas_call(
        kern,
        grid=(m // tm, ff // tf),
        in_specs=[
            pl.BlockSpec((tm, d), lambda i, f: (i, 0)),
            pl.BlockSpec((None, 1, d), lambda i, f: (layer, 0, 0)),
            pl.BlockSpec((d, tf), lambda i, f: (0, f)),
            pl.BlockSpec((tf, d), lambda i, f: (f, 0)),
            pl.BlockSpec((1, d), lambda i, f: (0, 0)),
        ],
        out_specs=pl.BlockSpec((tm, d), lambda i, f: (i, 0)),
        out_shape=jax.ShapeDtypeStruct((m, d), F32),
        scratch_shapes=[pltpu.VMEM((tm, d), BF16)],
        compiler_params=pltpu.CompilerParams(
            dimension_semantics=("arbitrary", "arbitrary"), vmem_limit_bytes=VMEM_LIMIT),
        name="mlp",
    )(x2, gain, w_up, w_down, final_gain)


def _tile(n, want):
    t = min(want, n)
    assert n % t == 0, (n, want)
    return t


def kernel(x, attn_norm, w_in, gdn_conv_w, gdn_a_log, gdn_dt_bias, gdn_norm, lru_conv_w, lru_conv_b,
           lru_w_a, lru_b_a, lru_w_x, lru_b_x, lru_lambda, w_branch_gdn, w_branch_lru, w_out,
           mlp_norm, w_up, w_down, final_norm):
    batch, seq, d = x.shape
    depth = w_in.shape[0]
    heads = gdn_a_log.shape[1]
    gw = heads * LANES
    assert gdn_norm.shape[1] == LANES and gdn_conv_w.shape[2] == 3 * gw
    blocks = lru_w_a.shape[1]
    lw = blocks * LANES
    assert lru_w_a.shape[2] == LANES
    assert 2 * heads <= LANES and seq % (GDN_TILES * GDN_PACK * GDN_CHUNK) == 0
    assert gw == lw
    m = batch * seq

    o_z = 3 * gw
    o_a = o_z + gw
    o_xb = o_a + 2 * heads
    o_yb = o_xb + lw
    o_gg = o_yb + lw
    o_gl = o_gg + d
    assert w_in.shape[2] == o_gl + d
    assert d == 2 * gw
    a_rows = (0, gw, 2 * gw, o_xb, o_gl)
    b_rows = (o_z, o_yb, o_gg, o_gg + gw, o_gl + gw)
    assert all(r % BF16_ROWS == 0 for r in a_rows + b_rows + (o_a,))
    gdn_cols = (0, 2, 4, 1)
    yb_col, xb_col = 3, 6
    gg_cols, gl_col = (5, 7), 4
    lru_step = 3

    tm_in = _tile(seq, 1024)
    ts_lru = _tile(seq, 1024)
    tm_merge = _tile(m, 256)
    tm_mlp = _tile(m, 1024)
    tf_mlp = _tile(w_up.shape[2], 1024)

    w_t = jnp.swapaxes(w_in, 1, 2).astype(BF16)
    zeros_w = jnp.zeros((depth, CONV_WIDTH, gw), F32)
    conv_w = jnp.concatenate([gdn_conv_w.astype(F32), lru_conv_w.astype(F32), zeros_w], axis=2)
    conv_b = jnp.pad(lru_conv_b.astype(F32), ((0, 0), (3 * gw, gw)))[:, None, :]
    pad_h = ((0, 0), (0, LANES - heads))
    alog_rows = jnp.pad(gdn_a_log.astype(F32), pad_h)[:, None, :]
    dtb_rows = jnp.pad(gdn_dt_bias.astype(F32), pad_h)[:, None, :]
    w_a_b, w_x_b = lru_w_a.astype(BF16), lru_w_x.astype(BF16)
    row3 = lambda v: v[:, None, :]

    x2 = x.reshape(m, d)
    for l in range(depth):
        proj, ab, w_bg_b, w_bl_b, w_out_b = _in_proj(
            x2, row3(attn_norm), w_t, conv_w, conv_b, (w_branch_gdn, w_branch_lru, w_out),
            l, tm_in, gw, seq, lru_step, a_rows, b_rows, o_a)
        o_gdn = _gdn(proj, ab, alog_rows, dtb_rows, row3(gdn_norm), l, batch, seq, heads, gdn_cols)
        o_lru = _lru(proj, w_a_b, row3(lru_b_a), w_x_b, row3(lru_b_x),
                     row3(lru_lambda), l, batch, seq, blocks, xb_col, yb_col, ts_lru)
        x2, w_up_b, w_down_b = _merge(x2, o_gdn, o_lru, proj, w_bg_b, w_bl_b, w_out_b, (w_up, w_down),
                                      l, gg_cols, gl_col, tm_merge)
        x2 = _mlp(x2, row3(mlp_norm), w_up_b, w_down_b, final_norm.reshape(1, d), l, l == depth - 1,
                  tm_mlp, tf_mlp)
    if depth == 0:
        raise ValueError("depth must be positive")
    return x2.reshape(batch, seq, d)
```

```python
import functools

import jax
import jax.numpy as jnp
from jax import lax
from jax.experimental import pallas as pl
from jax.experimental.pallas import tpu as pltpu

F32 = jnp.float32
BF16 = jnp.bfloat16

RMS_EPS = 1e-6
L2_EPS = 1e-6
LRU_C = 8.0
CONV_WIDTH = 4
GDN_CHUNK = 64
INV_BLOCK = 8
GDN_HEAD_GROUP = 8
GDN_PACK = 4
GDN_TILES = 4
CONV_COLS = 256
LANES = 128
SUBLANES = 8
BF16_ROWS = 16
VMEM_LIMIT = 60 * 1024 * 1024


def _mm(a, b):
    return jnp.dot(a.astype(BF16), b.astype(BF16), preferred_element_type=F32)


def _mm_tn(a, b):
    return lax.dot_general(a.astype(BF16), b.astype(BF16), (((0,), (0,)), ((), ())),
                           preferred_element_type=F32)


def _softplus(x):
    return jnp.maximum(x, 0.0) + jnp.log1p(jnp.exp(-jnp.abs(x)))


def _silu(x):
    h = 0.5 * x
    return h + h * jnp.tanh(h)


def _gelu_tanh(x):
    c = 0.7978845608028654
    return 0.5 * x * (1.0 + jnp.tanh(c * (x + 0.044715 * (x * x * x))))


def _cast_plan(w, steps):
    rows = w.shape[1]
    units = rows // BF16_ROWS
    assert rows % BF16_ROWS == 0
    nb = max(k for k in range(1, min(units, steps) + 1) if units % k == 0)
    return nb, rows // nb


def _cast_specs(w, layer, steps, step_of):
    nb, rb = _cast_plan(w, steps)
    cols = w.shape[2]
    blk = lambda *ids: jnp.minimum(step_of(*ids), nb - 1)
    return (pl.BlockSpec((None, rb, cols), lambda *ids: (layer, blk(*ids), 0)),
            pl.BlockSpec((rb, cols), lambda *ids: (blk(*ids), 0)),
            jax.ShapeDtypeStruct((w.shape[1], cols), BF16))


def _cast_step(srcs, dsts):
    for src, dst in zip(srcs, dsts):
        dst[...] = src[...].astype(dst.dtype)


def _dot_nt(a, b):
    return lax.dot_general(a, b, (((1,), (1,)), ((), ())), preferred_element_type=F32)


def _in_proj_kernel(x_ref, g_ref, wa_ref, wb_ref, wab_ref, cw_ref, cb_ref, o_ref, oab_ref, h_ref, tail_ref, *,
                    tiles_per_seq, lru_step):
    i = pl.program_id(0)
    j = pl.program_id(1)
    tm, tn = o_ref.shape
    half = tn // 2

    @pl.when(j == 0)
    def _():
        x = x_ref[...]
        ms = jnp.mean(x * x, axis=-1, keepdims=True)
        h = (x * lax.rsqrt(ms + RMS_EPS) * g_ref[...]).astype(BF16)
        h_ref[...] = h
        oab_ref[...] = _dot_nt(h, wab_ref[...])

        @pl.when(i == 0)
        def _():
            tail_ref[...] = jnp.zeros_like(tail_ref)

    cc = min(half, CONV_COLS)
    row8 = lax.broadcasted_iota(jnp.int32, (SUBLANES, cc), 0)

    def shift_rows(val, prev8, s):
        rolled = pltpu.roll(val, s, axis=0)
        top = jnp.where(row8 < s, pltpu.roll(prev8, s, axis=0), rolled[0:SUBLANES, :])
        return jnp.concatenate([top, rolled[SUBLANES:, :]], axis=0)

    def conv_chunks(finish):
        first = i % tiles_per_seq == 0
        nb = half // cc
        for ci in range(nb):
            c0 = ci * cc
            col = pl.multiple_of(j * half, half) + c0
            prev = jnp.where(first, 0.0, tail_ref[:, pl.ds(col, cc)])
            res = _dot_nt(h_ref[...], wa_ref[c0:c0 + cc, :])
            tail_ref[:, pl.ds(col, cc)] = res[tm - SUBLANES:tm, :]
            b0 = half + c0
            o_ref[:, b0:b0 + cc] = _dot_nt(h_ref[...], wb_ref[c0:c0 + cc, :]).astype(o_ref.dtype)
            taps = [cw_ref[k:k + 1, c0:c0 + cc] for k in range(CONV_WIDTH)]
            s1 = shift_rows(res, prev, 1)
            u = res * taps[3] + s1 * taps[2]
            v = res * taps[1] + s1 * taps[0]
            prev_v = prev * taps[1] + pltpu.roll(prev, 1, axis=0) * taps[0]
            finish(u + shift_rows(v, prev_v, 2), c0)

    @pl.when(j < 3)
    def _():
        is_v = j == 2
        q_scale = jnp.where(j == 0, LANES ** -0.5, 1.0)

        def finish(acc, c0):
            y = _silu(acc)
            for g in range(cc // LANES):
                yg = y[:, g * LANES:(g + 1) * LANES]
                inv = lax.rsqrt(jnp.sum(yg * yg, axis=-1, keepdims=True) + L2_EPS)
                o_ref[:, c0 + g * LANES:c0 + (g + 1) * LANES] = (
                    yg * jnp.where(is_v, 1.0, inv * q_scale)).astype(o_ref.dtype)

        conv_chunks(finish)

    @pl.when(j == lru_step)
    def _():
        def finish(acc, c0):
            o_ref[:, c0:c0 + cc] = (acc + cb_ref[:, c0:c0 + cc]).astype(o_ref.dtype)

        conv_chunks(finish)

    @pl.when((j >= 3) & (j != lru_step))
    def _():
        o_ref[:, 0:half] = _dot_nt(h_ref[...], wa_ref[...]).astype(o_ref.dtype)
        o_ref[:, half:tn] = _dot_nt(h_ref[...], wb_ref[...]).astype(o_ref.dtype)


def _in_proj(x2, gain, w_t, conv_w, conv_b, layer, tm, half, seq, lru_step, a_rows, b_rows, ab_row):
    m, d = x2.shape
    nj = len(a_rows)
    tn = 2 * half
    n = nj * tn
    kern = functools.partial(_in_proj_kernel, tiles_per_seq=seq // tm, lru_step=lru_step)

    def rows_of(table):
        return lambda i, j: (layer, BF16_ROWS * sum(jnp.where(j == s, r // BF16_ROWS, 0)
                                                     for s, r in enumerate(table)), 0)

    w_tile = (None, pl.Element(half), pl.Element(d))
    return pl.pallas_call(
        kern,
        grid=(m // tm, nj),
        in_specs=[
            pl.BlockSpec((tm, d), lambda i, j: (i, 0)),
            pl.BlockSpec((None, 1, d), lambda i, j: (layer, 0, 0)),
            pl.BlockSpec(w_tile, rows_of(a_rows)),
            pl.BlockSpec(w_tile, rows_of(b_rows)),
            pl.BlockSpec((None, pl.Element(LANES), pl.Element(d)), lambda i, j: (layer, ab_row, 0)),
            pl.BlockSpec((None, CONV_WIDTH, half), lambda i, j: (layer, 0, j)),
            pl.BlockSpec((None, 1, half), lambda i, j: (layer, 0, j)),
        ],
        out_specs=[
            pl.BlockSpec((tm, tn), lambda i, j: (i, j)),
            pl.BlockSpec((tm, LANES), lambda i, j: (i, 0)),
        ],
        out_shape=[
            jax.ShapeDtypeStruct((m, n), BF16),
            jax.ShapeDtypeStruct((m, LANES), F32),
        ],
        scratch_shapes=[pltpu.VMEM((tm, d), BF16), pltpu.VMEM((SUBLANES, n // 2), F32)],
        compiler_params=pltpu.CompilerParams(
            dimension_semantics=("arbitrary", "arbitrary"), vmem_limit_bytes=VMEM_LIMIT),
        name="in_proj",
    )(x2, gain, w_t, w_t, w_t, conv_w, conv_b)


def _gdn_kernel(q_ref, k_ref, v_ref, z_ref, ab_ref, alog_ref, dtb_ref, gain_ref, o_ref,
                state, qk_s, rhs_s, u_s, w_s, qd_s, kd_s, *, heads):
    ts = GDN_PACK * GDN_CHUNK

    @pl.when(pl.program_id(1) == 0)
    def _():
        state[...] = jnp.zeros_like(state)

    for tile in range(q_ref.shape[0] // ts):
        rows = pl.ds(tile * ts, ts)
        _gdn_tile(q_ref.at[rows], k_ref.at[rows], v_ref.at[rows], z_ref.at[rows], ab_ref.at[rows],
                  alog_ref, dtb_ref, gain_ref, o_ref.at[rows], state, qk_s, rhs_s, u_s, w_s, qd_s, kd_s, heads)


def _gdn_tile(q_ref, k_ref, v_ref, z_ref, ab_ref, alog_ref, dtb_ref, gain_ref, o_ref,
              state, qk_s, rhs_s, u_s, w_s, qd_s, kd_s, heads):
    hd = LANES
    width = heads * hd
    c = GDN_CHUNK
    ts = GDN_PACK * c
    hs = range(heads)

    ab = ab_ref[...]
    g_all = -jnp.exp(alog_ref[...]) * _softplus(ab + dtb_ref[...])
    beta_all = jax.nn.sigmoid(ab)

    r2 = lax.broadcasted_iota(jnp.int32, (ts, ts), 0)
    c2 = lax.broadcasted_iota(jnp.int32, (ts, ts), 1)
    same_chunk = (r2 & -c) == (c2 & -c)
    causal_bd = same_chunk & (r2 >= c2)
    strict_bd = same_chunk & (r2 > c2)
    rp = lax.broadcasted_iota(jnp.int32, (c, ts), 0)
    cp = lax.broadcasted_iota(jnp.int32, (c, ts), 1) & (c - 1)
    eye_p = jnp.where(rp == cp, 1.0, 0.0)
    in_diag_block = (rp & -INV_BLOCK) == (cp & -INV_BLOCK)

    lane = lax.broadcasted_iota(jnp.int32, (c, hd), 1)
    lane_masks = [jnp.where((lane & -c) == p * c, 1.0, 0.0).astype(BF16) for p in range(hd // c)]
    zero_blk = jnp.zeros((c, hd), BF16)

    def blockdiag(y):
        yb = y.astype(BF16)
        rows = []
        for i in range(GDN_PACK):
            slab, off = divmod(i * c, hd)
            blocks = [zero_blk] * (ts // hd)
            blocks[slab] = yb[:, slab * hd:(slab + 1) * hd] * lane_masks[off // c]
            rows.append(jnp.concatenate(blocks, axis=1))
        return jnp.concatenate(rows, axis=0)

    def bmm(x, bd):
        return jnp.dot(x.astype(BF16), bd, preferred_element_type=F32)

    g_hi = g_all.astype(BF16)
    r_mid = g_all - g_hi.astype(F32)
    g_mid = r_mid.astype(BF16)
    g_lo = (r_mid - g_mid.astype(F32)).astype(BF16)
    sums = jnp.dot(jnp.where(causal_bd, 1.0, 0.0).astype(BF16), jnp.concatenate([g_hi, g_mid, g_lo], axis=1),
                   preferred_element_type=F32)
    gcum = sums[:, 0:LANES] + sums[:, LANES:2 * LANES] + sums[:, 2 * LANES:3 * LANES]
    glast = jnp.concatenate([jnp.broadcast_to(gcum[(ci + 1) * c - 1:(ci + 1) * c, :], (c, LANES))
                             for ci in range(GDN_PACK)], axis=0)
    gcum_t = gcum.T

    for g0 in range(0, heads, GDN_HEAD_GROUP):
        hg = range(g0, min(g0 + GDN_HEAD_GROUP, heads))
        prods = []
        for h in hg:
            qb = q_ref[:, h * hd:(h + 1) * hd]
            kb16 = k_ref[:, h * hd:(h + 1) * hd]
            q = qb.astype(F32)
            k = kb16.astype(F32)
            v = v_ref[:, h * hd:(h + 1) * hd].astype(F32)
            gcol = gcum[:, h:h + 1]
            egc = jnp.exp(gcol)
            beta = beta_all[:, heads + h:heads + h + 1]
            kb = k * beta
            prods.append(lax.dot_general(jnp.concatenate([qb, kb.astype(BF16)], axis=0), kb16,
                                         (((1,), (1,)), ((), ())), preferred_element_type=F32))
            rhs_s[h] = jnp.concatenate([v * beta, kb * egc], axis=1).astype(BF16)
            qd_s[h] = (q * egc).astype(BF16)
            kd_s[h] = (k * jnp.exp(glast[:, h:h + 1] - gcol)).astype(BF16)
        lps = []
        for n, h in enumerate(hg):
            gcol = gcum[:, h:h + 1]
            grow = gcum_t[h:h + 1, :]
            dmat = jnp.where(causal_bd, jnp.exp(jnp.where(causal_bd, gcol - grow, 0.0)), 0.0)
            qk_s[h] = (prods[n][0:ts] * dmat).astype(BF16)
            lbd = jnp.where(strict_bd, prods[n][ts:2 * ts] * dmat, 0.0)
            lp = lbd[0:c]
            for i in range(1, GDN_PACK):
                lp = lp + lbd[i * c:(i + 1) * c]
            lps.append(lp)

        ld = [jnp.where(in_diag_block, lp, 0.0) for lp in lps]
        tinv = [eye_p - d for d in ld]
        pw = [bmm(d, blockdiag(d)) for d in ld]
        span = 2
        while span < INV_BLOCK:
            last = 2 * span >= INV_BLOCK
            res = [bmm(d if last else jnp.concatenate([p, d], axis=0), blockdiag(p)) for p, d in zip(pw, tinv)]
            if last:
                tinv = [d + r for d, r in zip(tinv, res)]
            else:
                pw = [r[0:c] for r in res]
                tinv = [d + r[c:2 * c] for d, r in zip(tinv, res)]
            span *= 2
        b = INV_BLOCK
        while b < c:
            off = ((rp & -2 * b) == (cp & -2 * b)) & ((rp & -b) != (cp & -b))
            te = [bmm(x, blockdiag(jnp.where(off, lp, 0.0))) for x, lp in zip(tinv, lps)]
            tinv = [x - bmm(y, blockdiag(x)) for x, y in zip(tinv, te)]
            b *= 2

        for n, h in enumerate(hg):
            uw = jnp.dot(blockdiag(tinv[n]), rhs_s[h], preferred_element_type=F32)
            u_s[h] = uw[:, 0:hd]
            w_s[h] = uw[:, hd:2 * hd].astype(BF16)

    gain = gain_ref[...]
    zero_v = jnp.zeros((c, hd), BF16)
    for ci in range(GDN_PACK):
        r0 = ci * c
        slab = (r0 // hd) * hd
        wss = [_mm(jnp.concatenate([w_s[h, r0:r0 + c, :], qd_s[h, r0:r0 + c, :]], axis=0), state[h]) for h in hs]
        for h in hs:
            v_new = u_s[h, r0:r0 + c, :] - wss[h][0:c]
            v_new_b = v_new.astype(BF16)
            pieces = [zero_v] * (hd // c)
            pieces[(r0 - slab) // c] = v_new_b
            out = wss[h][c:2 * c] + jnp.dot(qk_s[h, r0:r0 + c, slab:slab + hd], jnp.concatenate(pieces, axis=0),
                                            preferred_element_type=F32)
            state[h] = (state[h] * jnp.exp(glast[r0:r0 + 1, h:h + 1])
                        + _mm_tn(kd_s[h, r0:r0 + c, :], v_new_b))
            o = out * lax.rsqrt(jnp.mean(out * out, axis=-1, keepdims=True) + RMS_EPS) * gain
            zz = z_ref[r0:r0 + c, h * hd:(h + 1) * hd].astype(F32)
            o_ref[r0:r0 + c, h * hd:(h + 1) * hd] = (o * _silu(zz)).astype(o_ref.dtype)


def _gdn(proj, ab, alog_row, dtb_row, gain_row, layer, batch, seq, heads, cols):
    width = heads * LANES
    tt = GDN_PACK * GDN_CHUNK
    ts = GDN_TILES * tt
    nt = seq // ts
    kern = functools.partial(_gdn_kernel, heads=heads)
    return pl.pallas_call(
        kern,
        grid=(batch, nt),
        in_specs=[
            pl.BlockSpec((ts, width), lambda b, t: (b * nt + t, cols[0])),
            pl.BlockSpec((ts, width), lambda b, t: (b * nt + t, cols[1])),
            pl.BlockSpec((ts, width), lambda b, t: (b * nt + t, cols[2])),
            pl.BlockSpec((ts, width), lambda b, t: (b * nt + t, cols[3])),
            pl.BlockSpec((ts, LANES), lambda b, t: (b * nt + t, 0)),
            pl.BlockSpec((None, 1, LANES), lambda b, t: (layer, 0, 0)),
            pl.BlockSpec((None, 1, LANES), lambda b, t: (layer, 0, 0)),
            pl.BlockSpec((None, 1, LANES), lambda b, t: (layer, 0, 0)),
        ],
        out_specs=pl.BlockSpec((ts, width), lambda b, t: (b * nt + t, 0)),
        out_shape=jax.ShapeDtypeStruct((batch * seq, width), BF16),
        scratch_shapes=[
            pltpu.VMEM((heads, LANES, LANES), F32),
            pltpu.VMEM((heads, tt, tt), BF16),
            pltpu.VMEM((heads, tt, 2 * LANES), BF16),
            pltpu.VMEM((heads, tt, LANES), F32),
            pltpu.VMEM((heads, tt, LANES), BF16),
            pltpu.VMEM((heads, tt, LANES), BF16),
            pltpu.VMEM((heads, tt, LANES), BF16),
        ],
        compiler_params=pltpu.CompilerParams(
            dimension_semantics=("arbitrary", "arbitrary"), vmem_limit_bytes=VMEM_LIMIT),
        name="gdn",
    )(proj, proj, proj, proj, ab, alog_row, dtb_row, gain_row)


def _lru_kernel(xc_ref, yb_ref, wa_ref, ba_ref, wx_ref, bx_ref, lam_ref, c0_ref, c1_ref, c2_ref,
                o_ref, d0_ref, d1_ref, d2_ref, abuf, bbuf, hcar, *, blocks, ts):
    bd = LANES
    width = blocks * bd
    _cast_step((c0_ref, c1_ref, c2_ref), (d0_ref, d1_ref, d2_ref))

    @pl.when(pl.program_id(1) == 0)
    def _():
        hcar[...] = jnp.zeros_like(hcar)

    for n in range(blocks):
        lo = n * bd
        xcb = xc_ref[:, lo:lo + bd]
        xc = xcb.astype(F32)
        r = jax.nn.sigmoid(jnp.dot(xcb, wa_ref[n], preferred_element_type=F32) + ba_ref[0:1, lo:lo + bd])
        i = jax.nn.sigmoid(jnp.dot(xcb, wx_ref[n], preferred_element_type=F32) + bx_ref[0:1, lo:lo + bd])
        log_a = (-LRU_C) * r * _softplus(-lam_ref[0:1, lo:lo + bd])
        a = jnp.exp(log_a)
        one_minus_a2 = -jnp.tanh(log_a) * (1.0 + a * a)
        abuf[:, lo:lo + bd] = a
        bbuf[:, lo:lo + bd] = jnp.sqrt(one_minus_a2) * (i * xc)

    rowi = lax.broadcasted_iota(jnp.int32, (SUBLANES, width), 0)

    def group(gi, hprev):
        r0 = pl.multiple_of(gi * SUBLANES, SUBLANES)
        a = abuf[pl.ds(r0, SUBLANES), :]
        b = bbuf[pl.ds(r0, SUBLANES), :]
        s = 1
        while s < SUBLANES:
            keep = rowi >= s
            a_sh = jnp.where(keep, pltpu.roll(a, s, axis=0), 1.0)
            b_sh = jnp.where(keep, pltpu.roll(b, s, axis=0), 0.0)
            b = a * b_sh + b
            a = a * a_sh
            s *= 2
        bbuf[pl.ds(r0, SUBLANES), :] = a * hprev + b
        return a[SUBLANES - 1:SUBLANES, :] * hprev + b[SUBLANES - 1:SUBLANES, :]

    hcar[...] = lax.fori_loop(0, ts // SUBLANES, group, hcar[...], unroll=4)
    o_ref[...] = (bbuf[...] * _gelu_tanh(yb_ref[...].astype(F32))).astype(o_ref.dtype)


def _lru(proj, w_a, b_a, w_x, b_x, lam, casts, layer, batch, seq, blocks, xb_col, yb_col, ts):
    width = blocks * LANES
    nt = seq // ts
    cast = [_cast_specs(w, layer, batch * nt, lambda b, t: b * nt + t) for w in casts]
    kern = functools.partial(_lru_kernel, blocks=blocks, ts=ts)
    row = lambda b, t: (layer, 0, 0)
    mat = lambda b, t: (layer, 0, 0, 0)
    return pl.pallas_call(
        kern,
        grid=(batch, nt),
        in_specs=[
            pl.BlockSpec((ts, width), lambda b, t: (b * nt + t, xb_col)),
            pl.BlockSpec((ts, width), lambda b, t: (b * nt + t, yb_col)),
            pl.BlockSpec((None, blocks, LANES, LANES), mat),
            pl.BlockSpec((None, 1, width), row),
            pl.BlockSpec((None, blocks, LANES, LANES), mat),
            pl.BlockSpec((None, 1, width), row),
            pl.BlockSpec((None, 1, width), row),
        ] + [c[0] for c in cast],
        out_specs=[pl.BlockSpec((ts, width), lambda b, t: (b * nt + t, 0))] + [c[1] for c in cast],
        out_shape=[jax.ShapeDtypeStruct((batch * seq, width), BF16)] + [c[2] for c in cast],
        scratch_shapes=[
            pltpu.VMEM((ts, width), F32),
            pltpu.VMEM((ts, width), F32),
            pltpu.VMEM((1, width), F32),
        ],
        compiler_params=pltpu.CompilerParams(
            dimension_semantics=("arbitrary", "arbitrary"), vmem_limit_bytes=VMEM_LIMIT),
        name="lru",
    )(proj, proj, w_a, b_a, w_x, b_x, lam, *casts)


def _merge_kernel(x_ref, og_ref, ol_ref, gg0_ref, gg1_ref, gl_ref, wg_ref, wl_ref, wo_ref, c0_ref, c1_ref,
                  o_ref, d0_ref, d1_ref):
    _cast_step((c0_ref, c1_ref), (d0_ref, d1_ref))
    pg = jnp.dot(og_ref[...], wg_ref[...], preferred_element_type=F32)
    pb = jnp.dot(ol_ref[...], wl_ref[...], preferred_element_type=F32)
    gg = jnp.concatenate([gg0_ref[...], gg1_ref[...]], axis=1)
    merged = (jax.nn.sigmoid(gg.astype(F32)) * pg
              + jax.nn.sigmoid(gl_ref[...].astype(F32)) * pb)
    o_ref[...] = x_ref[...] + jnp.dot(merged.astype(BF16), wo_ref[...], preferred_element_type=F32)


def _merge(x2, o_gdn, o_lru, proj, w_bg, w_bl, w_out, casts, layer, gg_cols, gl_col, tm):
    m, d = x2.shape
    wg = o_gdn.shape[1]
    wl = o_lru.shape[1]
    const = lambda i: (0, 0)
    steps = m // tm
    cast = [_cast_specs(w, layer, steps, lambda i: i) for w in casts]
    return pl.pallas_call(
        _merge_kernel,
        grid=(steps,),
        in_specs=[
            pl.BlockSpec((tm, d), lambda i: (i, 0)),
            pl.BlockSpec((tm, wg), lambda i: (i, 0)),
            pl.BlockSpec((tm, wl), lambda i: (i, 0)),
            pl.BlockSpec((tm, d // 2), lambda i: (i, gg_cols[0])),
            pl.BlockSpec((tm, d // 2), lambda i: (i, gg_cols[1])),
            pl.BlockSpec((tm, d), lambda i: (i, gl_col)),
            pl.BlockSpec((wg, d), const, pipeline_mode=pl.Buffered(1)),
            pl.BlockSpec((wl, d), const, pipeline_mode=pl.Buffered(1)),
            pl.BlockSpec((d, d), const, pipeline_mode=pl.Buffered(1)),
        ] + [c[0] for c in cast],
        out_specs=[pl.BlockSpec((tm, d), lambda i: (i, 0))] + [c[1] for c in cast],
        out_shape=[jax.ShapeDtypeStruct((m, d), F32)] + [c[2] for c in cast],
        compiler_params=pltpu.CompilerParams(
            dimension_semantics=("arbitrary",), vmem_limit_bytes=VMEM_LIMIT),
        name="merge",
    )(x2, o_gdn, o_lru, proj, proj, proj, w_bg, w_bl, w_out, *casts)


def _mlp_kernel(x_ref, g_ref, wu_ref, wd_ref, fg_ref, o_ref, h_ref, *, final_norm):
    f = pl.program_id(1)

    @pl.when(f == 0)
    def _():
        x = x_ref[...]
        ms = jnp.mean(x * x, axis=-1, keepdims=True)
        h_ref[...] = (x * lax.rsqrt(ms + RMS_EPS) * g_ref[...]).astype(BF16)
        o_ref[...] = x

    u = jnp.dot(h_ref[...], wu_ref[...], preferred_element_type=F32)
    u = jnp.maximum(u, 0.0)
    o_ref[...] += jnp.dot((u * u).astype(BF16), wd_ref[...], preferred_element_type=F32)

    if final_norm:
        @pl.when(f == pl.num_programs(1) - 1)
        def _():
            y = o_ref[...]
            ms = jnp.mean(y * y, axis=-1, keepdims=True)
            o_ref[...] = y * lax.rsqrt(ms + RMS_EPS) * fg_ref[...]


def _mlp(x2, gain, w_up, w_down, final_gain, layer, final_norm, tm, tf):
    m, d = x2.shape
    ff = w_up.shape[1]
    kern = functools.partial(_mlp_kernel, final_norm=final_norm)
    return pl.pallas_call(
        kern,
        grid=(m // tm, ff // tf),
        in_specs=[
            pl.BlockSpec((tm, d), lambda i, f: (i, 0)),
            pl.BlockSpec((None, 1, d), lambda i, f: (layer, 0, 0)),
            pl.BlockSpec((d, tf), lambda i, f: (0, f)),
            pl.BlockSpec((tf, d), lambda i, f: (f, 0)),
            pl.BlockSpec((1, d), lambda i, f: (0, 0)),
        ],
        out_specs=pl.BlockSpec((tm, d), lambda i, f: (i, 0)),
        out_shape=jax.ShapeDtypeStruct((m, d), F32),
        scratch_shapes=[pltpu.VMEM((tm, d), BF16)],
        compiler_params=pltpu.CompilerParams(
            dimension_semantics=("arbitrary", "arbitrary"), vmem_limit_bytes=VMEM_LIMIT),
        name="mlp",
    )(x2, gain, w_up, w_down, final_gain)


def _tile(n, want):
    t = min(want, n)
    assert n % t == 0, (n, want)
    return t


def kernel(x, attn_norm, w_in, gdn_conv_w, gdn_a_log, gdn_dt_bias, gdn_norm, lru_conv_w, lru_conv_b,
           lru_w_a, lru_b_a, lru_w_x, lru_b_x, lru_lambda, w_branch_gdn, w_branch_lru, w_out,
           mlp_norm, w_up, w_down, final_norm):
    batch, seq, d = x.shape
    depth = w_in.shape[0]
    heads = gdn_a_log.shape[1]
    gw = heads * LANES
    assert gdn_norm.shape[1] == LANES and gdn_conv_w.shape[2] == 3 * gw
    blocks = lru_w_a.shape[1]
    lw = blocks * LANES
    assert lru_w_a.shape[2] == LANES
    assert 2 * heads <= LANES and seq % (GDN_TILES * GDN_PACK * GDN_CHUNK) == 0
    assert gw == lw
    m = batch * seq

    o_z = 3 * gw
    o_a = o_z + gw
    o_xb = o_a + 2 * heads
    o_yb = o_xb + lw
    o_gg = o_yb + lw
    o_gl = o_gg + d
    assert w_in.shape[2] == o_gl + d
    assert d == 2 * gw
    a_rows = (0, gw, 2 * gw, o_xb, o_gl)
    b_rows = (o_z, o_yb, o_gg, o_gg + gw, o_gl + gw)
    assert all(r % BF16_ROWS == 0 for r in a_rows + b_rows + (o_a,))
    gdn_cols = (0, 2, 4, 1)
    yb_col, xb_col = 3, 6
    gg_cols, gl_col = (5, 7), 4
    lru_step = 3

    tm_in = _tile(seq, 1024)
    ts_lru = _tile(seq, 1024)
    tm_merge = _tile(m, 256)
    tm_mlp = _tile(m, 1024)
    tf_mlp = _tile(w_up.shape[2], 1024)

    w_t = jnp.swapaxes(w_in, 1, 2).astype(BF16)
    zeros_w = jnp.zeros((depth, CONV_WIDTH, gw), F32)
    conv_w = jnp.concatenate([gdn_conv_w.astype(F32), lru_conv_w.astype(F32), zeros_w], axis=2)
    conv_b = jnp.pad(lru_conv_b.astype(F32), ((0, 0), (3 * gw, gw)))[:, None, :]
    pad_h = ((0, 0), (0, LANES - heads))
    alog_rows = jnp.pad(gdn_a_log.astype(F32), pad_h)[:, None, :]
    dtb_rows = jnp.pad(gdn_dt_bias.astype(F32), pad_h)[:, None, :]
    w_a_b, w_x_b = lru_w_a.astype(BF16), lru_w_x.astype(BF16)
    row3 = lambda v: v[:, None, :]

    x2 = x.reshape(m, d)
    for l in range(depth):
        proj, ab = _in_proj(x2, row3(attn_norm), w_t, conv_w, conv_b, l, tm_in, gw, seq, lru_step,
                            a_rows, b_rows, o_a)
        o_gdn = _gdn(proj, ab, alog_rows, dtb_rows, row3(gdn_norm), l, batch, seq, heads, gdn_cols)
        o_lru, w_bg_b, w_bl_b, w_out_b = _lru(proj, w_a_b, row3(lru_b_a), w_x_b, row3(lru_b_x), row3(lru_lambda),
                                              (w_branch_gdn, w_branch_lru, w_out), l, batch, seq, blocks,
                                              xb_col, yb_col, ts_lru)
        x2, w_up_b, w_down_b = _merge(x2, o_gdn, o_lru, proj, w_bg_b, w_bl_b, w_out_b, (w_up, w_down),
                                      l, gg_cols, gl_col, tm_merge)
        x2 = _mlp(x2, row3(mlp_norm), w_up_b, w_down_b, final_norm.reshape(1, d), l, l == depth - 1,
                  tm_mlp, tf_mlp)
    if depth == 0:
        raise ValueError("depth must be positive")
    return x2.reshape(batch, seq, d)
```

```python
import functools

import jax
import jax.numpy as jnp
from jax import lax
from jax.experimental import pallas as pl
from jax.experimental.pallas import tpu as pltpu

F32 = jnp.float32
BF16 = jnp.bfloat16

RMS_EPS = 1e-6
L2_EPS = 1e-6
LRU_C = 8.0
CONV_WIDTH = 4
GDN_CHUNK = 64
INV_BLOCK = 8
GDN_HEAD_GROUP = 8
GDN_PACK = 4
GDN_TILES = 2
CONV_COLS = 256
LANES = 128
SUBLANES = 8
BF16_ROWS = 16
VMEM_LIMIT = 60 * 1024 * 1024


def _mm(a, b):
    return jnp.dot(a.astype(BF16), b.astype(BF16), preferred_element_type=F32)


def _mm_tn(a, b):
    return lax.dot_general(a.astype(BF16), b.astype(BF16), (((0,), (0,)), ((), ())),
                           preferred_element_type=F32)


def _softplus(x):
    return jnp.maximum(x, 0.0) + jnp.log1p(jnp.exp(-jnp.abs(x)))


def _silu(x):
    h = 0.5 * x
    return h + h * jnp.tanh(h)


def _gelu_tanh(x):
    c = 0.7978845608028654
    return 0.5 * x * (1.0 + jnp.tanh(c * (x + 0.044715 * (x * x * x))))


def _cast_plan(w, steps):
    rows = w.shape[1]
    units = rows // BF16_ROWS
    assert rows % BF16_ROWS == 0
    nb = max(k for k in range(1, min(units, steps) + 1) if units % k == 0)
    return nb, rows // nb


def _cast_specs(w, layer, steps, step_of):
    nb, rb = _cast_plan(w, steps)
    cols = w.shape[2]
    blk = lambda *ids: jnp.minimum(step_of(*ids), nb - 1)
    return (pl.BlockSpec((None, rb, cols), lambda *ids: (layer, blk(*ids), 0)),
            pl.BlockSpec((rb, cols), lambda *ids: (blk(*ids), 0)),
            jax.ShapeDtypeStruct((w.shape[1], cols), BF16))


def _cast_step(srcs, dsts):
    for src, dst in zip(srcs, dsts):
        dst[...] = src[...].astype(dst.dtype)


def _dot_nt(a, b):
    return lax.dot_general(a, b, (((1,), (1,)), ((), ())), preferred_element_type=F32)


def _in_proj_kernel(x_ref, g_ref, wa_ref, wb_ref, wab_ref, cw_ref, cb_ref, c0_ref, c1_ref, c2_ref,
                    o_ref, oab_ref, d0_ref, d1_ref, d2_ref, h_ref, tail_ref, *,
                    tiles_per_seq, lru_step):
    i = pl.program_id(0)
    j = pl.program_id(1)
    tm, tn = o_ref.shape
    half = tn // 2

    def ride_casts():
        _cast_step((c0_ref, c1_ref, c2_ref), (d0_ref, d1_ref, d2_ref))

    @pl.when(j == 0)
    def _():
        x = x_ref[...]
        ms = jnp.mean(x * x, axis=-1, keepdims=True)
        h = (x * lax.rsqrt(ms + RMS_EPS) * g_ref[0:1, :]).astype(BF16)
        h_ref[...] = h
        oab_ref[...] = _dot_nt(h, wab_ref[...])

        @pl.when(i == 0)
        def _():
            tail_ref[...] = jnp.zeros_like(tail_ref)

    cc = min(half, CONV_COLS)
    row8 = lax.broadcasted_iota(jnp.int32, (SUBLANES, cc), 0)

    def shift_rows(val, prev8, s):
        rolled = pltpu.roll(val, s, axis=0)
        top = jnp.where(row8 < s, pltpu.roll(prev8, s, axis=0), rolled[0:SUBLANES, :])
        return jnp.concatenate([top, rolled[SUBLANES:, :]], axis=0)

    def conv_chunks(finish):
        ride_casts()
        first = i % tiles_per_seq == 0
        nb = half // cc
        for ci in range(nb):
            c0 = ci * cc
            col = pl.multiple_of(j * half, half) + c0
            prev = jnp.where(first, 0.0, tail_ref[:, pl.ds(col, cc)])
            res = _dot_nt(h_ref[...], wa_ref[c0:c0 + cc, :])
            tail_ref[:, pl.ds(col, cc)] = res[tm - SUBLANES:tm, :]
            b0 = half + c0
            o_ref[:, b0:b0 + cc] = _dot_nt(h_ref[...], wb_ref[c0:c0 + cc, :]).astype(o_ref.dtype)
            taps = [cw_ref[k:k + 1, c0:c0 + cc] for k in range(CONV_WIDTH)]
            s1 = shift_rows(res, prev, 1)
            u = res * taps[3] + s1 * taps[2]
            v = res * taps[1] + s1 * taps[0]
            prev_v = prev * taps[1] + pltpu.roll(prev, 1, axis=0) * taps[0]
            finish(u + shift_rows(v, prev_v, 2), c0)

    @pl.when(j < 3)
    def _():
        is_v = j == 2
        q_scale = jnp.where(j == 0, LANES ** -0.5, 1.0)

        def finish(acc, c0):
            y = _silu(acc)
            for g in range(cc // LANES):
                yg = y[:, g * LANES:(g + 1) * LANES]
                inv = lax.rsqrt(jnp.sum(yg * yg, axis=-1, keepdims=True) + L2_EPS)
                o_ref[:, c0 + g * LANES:c0 + (g + 1) * LANES] = (
                    yg * jnp.where(is_v, 1.0, inv * q_scale)).astype(o_ref.dtype)

        conv_chunks(finish)

    @pl.when(j == lru_step)
    def _():
        def finish(acc, c0):
            o_ref[:, c0:c0 + cc] = (acc + cb_ref[0:1, c0:c0 + cc]).astype(o_ref.dtype)

        conv_chunks(finish)

    @pl.when((j >= 3) & (j != lru_step))
    def _():
        ride_casts()
        o_ref[:, 0:half] = _dot_nt(h_ref[...], wa_ref[...]).astype(o_ref.dtype)
        o_ref[:, half:tn] = _dot_nt(h_ref[...], wb_ref[...]).astype(o_ref.dtype)


def _in_proj(x2, gain, w_t, conv_w, conv_b, casts, layer, tm, half, seq, lru_step, a_rows, b_rows, ab_row):
    m, d = x2.shape
    nj = len(a_rows)
    tn = 2 * half
    n = nj * tn
    steps = (m // tm) * nj
    cast = [_cast_specs(w, layer, steps, lambda i, j: i * nj + j) for w in casts]
    kern = functools.partial(_in_proj_kernel, tiles_per_seq=seq // tm, lru_step=lru_step)

    def rows_of(table):
        return lambda i, j: (layer, BF16_ROWS * sum(jnp.where(j == s, r // BF16_ROWS, 0)
                                                     for s, r in enumerate(table)), 0)

    w_tile = (None, pl.Element(half), pl.Element(d))
    return pl.pallas_call(
        kern,
        grid=(m // tm, nj),
        in_specs=[
            pl.BlockSpec((tm, d), lambda i, j: (i, 0)),
            pl.BlockSpec((None, SUBLANES, d), lambda i, j: (layer, 0, 0)),
            pl.BlockSpec(w_tile, rows_of(a_rows)),
            pl.BlockSpec(w_tile, rows_of(b_rows)),
            pl.BlockSpec((None, pl.Element(LANES), pl.Element(d)), lambda i, j: (layer, ab_row, 0)),
            pl.BlockSpec((None, SUBLANES, half), lambda i, j: (layer, 0, j)),
            pl.BlockSpec((None, SUBLANES, half), lambda i, j: (layer, 0, j)),
        ] + [c[0] for c in cast],
        out_specs=[
            pl.BlockSpec((tm, tn), lambda i, j: (i, j)),
            pl.BlockSpec((tm, LANES), lambda i, j: (i, 0)),
        ] + [c[1] for c in cast],
        out_shape=[
            jax.ShapeDtypeStruct((m, n), BF16),
            jax.ShapeDtypeStruct((m, LANES), F32),
        ] + [c[2] for c in cast],
        scratch_shapes=[pltpu.VMEM((tm, d), BF16), pltpu.VMEM((SUBLANES, n // 2), F32)],
        compiler_params=pltpu.CompilerParams(
            dimension_semantics=("arbitrary", "arbitrary"), vmem_limit_bytes=VMEM_LIMIT),
        name="in_proj",
    )(x2, gain, w_t, w_t, w_t, conv_w, conv_b, *casts)


def _gdn_kernel(q_ref, k_ref, v_ref, z_ref, ab_ref, alog_ref, dtb_ref, gain_ref, o_ref,
                state, qk_s, rhs_s, u_s, w_s, qd_s, kd_s, *, heads):
    ts = GDN_PACK * GDN_CHUNK

    @pl.when(pl.program_id(1) == 0)
    def _():
        state[...] = jnp.zeros_like(state)

    for tile in range(q_ref.shape[0] // ts):
        rows = pl.ds(tile * ts, ts)
        _gdn_tile(q_ref.at[rows], k_ref.at[rows], v_ref.at[rows], z_ref.at[rows], ab_ref.at[rows],
                  alog_ref, dtb_ref, gain_ref, o_ref.at[rows], state, qk_s, rhs_s, u_s, w_s, qd_s, kd_s, heads)


def _gdn_tile(q_ref, k_ref, v_ref, z_ref, ab_ref, alog_ref, dtb_ref, gain_ref, o_ref,
              state, qk_s, rhs_s, u_s, w_s, qd_s, kd_s, heads):
    hd = LANES
    width = heads * hd
    c = GDN_CHUNK
    ts = GDN_PACK * c
    hs = range(heads)

    ab = ab_ref[...]
    g_all = -jnp.exp(alog_ref[...]) * _softplus(ab + dtb_ref[...])
    beta_all = jax.nn.sigmoid(ab)

    r2 = lax.broadcasted_iota(jnp.int32, (ts, ts), 0)
    c2 = lax.broadcasted_iota(jnp.int32, (ts, ts), 1)
    same_chunk = (r2 & -c) == (c2 & -c)
    causal_bd = same_chunk & (r2 >= c2)
    strict_bd = same_chunk & (r2 > c2)
    rp = lax.broadcasted_iota(jnp.int32, (c, ts), 0)
    cp = lax.broadcasted_iota(jnp.int32, (c, ts), 1) & (c - 1)
    eye_p = jnp.where(rp == cp, 1.0, 0.0)
    in_diag_block = (rp & -INV_BLOCK) == (cp & -INV_BLOCK)

    lane = lax.broadcasted_iota(jnp.int32, (c, hd), 1)
    lane_masks = [jnp.where((lane & -c) == p * c, 1.0, 0.0).astype(BF16) for p in range(hd // c)]
    zero_blk = jnp.zeros((c, hd), BF16)

    def blockdiag(y):
        yb = y.astype(BF16)
        rows = []
        for i in range(GDN_PACK):
            slab, off = divmod(i * c, hd)
            blocks = [zero_blk] * (ts // hd)
            blocks[slab] = yb[:, slab * hd:(slab + 1) * hd] * lane_masks[off // c]
            rows.append(jnp.concatenate(blocks, axis=1))
        return jnp.concatenate(rows, axis=0)

    def bmm(x, bd):
        return jnp.dot(x.astype(BF16), bd, preferred_element_type=F32)

    g_hi = g_all.astype(BF16)
    r_mid = g_all - g_hi.astype(F32)
    g_mid = r_mid.astype(BF16)
    g_lo = (r_mid - g_mid.astype(F32)).astype(BF16)
    sums = jnp.dot(jnp.where(causal_bd, 1.0, 0.0).astype(BF16), jnp.concatenate([g_hi, g_mid, g_lo], axis=1),
                   preferred_element_type=F32)
    gcum = sums[:, 0:LANES] + sums[:, LANES:2 * LANES] + sums[:, 2 * LANES:3 * LANES]
    glast = jnp.concatenate([jnp.broadcast_to(gcum[(ci + 1) * c - 1:(ci + 1) * c, :], (c, LANES))
                             for ci in range(GDN_PACK)], axis=0)
    gcum_t = gcum.T

    for g0 in range(0, heads, GDN_HEAD_GROUP):
        hg = range(g0, min(g0 + GDN_HEAD_GROUP, heads))
        prods = []
        for h in hg:
            qb = q_ref[:, h * hd:(h + 1) * hd]
            kb16 = k_ref[:, h * hd:(h + 1) * hd]
            q = qb.astype(F32)
            k = kb16.astype(F32)
            v = v_ref[:, h * hd:(h + 1) * hd].astype(F32)
            gcol = gcum[:, h:h + 1]
            egc = jnp.exp(gcol)
            beta = beta_all[:, heads + h:heads + h + 1]
            kb = k * beta
            prods.append(lax.dot_general(jnp.concatenate([qb, kb.astype(BF16)], axis=0), kb16,
                                         (((1,), (1,)), ((), ())), preferred_element_type=F32))
            rhs_s[h] = jnp.concatenate([v * beta, kb * egc], axis=1).astype(BF16)
            qd_s[h] = (q * egc).astype(BF16)
            kd_s[h] = (k * jnp.exp(glast[:, h:h + 1] - gcol)).astype(BF16)
        lps = []
        for n, h in enumerate(hg):
            gcol = gcum[:, h:h + 1]
            grow = gcum_t[h:h + 1, :]
            dmat = jnp.where(causal_bd, jnp.exp(jnp.where(causal_bd, gcol - grow, 0.0)), 0.0)
            qk_s[h] = (prods[n][0:ts] * dmat).astype(BF16)
            lbd = jnp.where(strict_bd, prods[n][ts:2 * ts] * dmat, 0.0)
            lp = lbd[0:c]
            for i in range(1, GDN_PACK):
                lp = lp + lbd[i * c:(i + 1) * c]
            lps.append(lp)

        ld = [jnp.where(in_diag_block, lp, 0.0) for lp in lps]
        tinv = [eye_p - d for d in ld]
        pw = [bmm(d, blockdiag(d)) for d in ld]
        span = 2
        while span < INV_BLOCK:
            last = 2 * span >= INV_BLOCK
            res = [bmm(d if last else jnp.concatenate([p, d], axis=0), blockdiag(p)) for p, d in zip(pw, tinv)]
            if last:
                tinv = [d + r for d, r in zip(tinv, res)]
            else:
                pw = [r[0:c] for r in res]
                tinv = [d + r[c:2 * c] for d, r in zip(tinv, res)]
            span *= 2
        b = INV_BLOCK
        while b < c:
            off = ((rp & -2 * b) == (cp & -2 * b)) & ((rp & -b) != (cp & -b))
            te = [bmm(x, blockdiag(jnp.where(off, lp, 0.0))) for x, lp in zip(tinv, lps)]
            tinv = [x - bmm(y, blockdiag(x)) for x, y in zip(tinv, te)]
            b *= 2

        for n, h in enumerate(hg):
            uw = jnp.dot(blockdiag(tinv[n]), rhs_s[h], preferred_element_type=F32)
            u_s[h] = uw[:, 0:hd]
            w_s[h] = uw[:, hd:2 * hd].astype(BF16)

    gain = gain_ref[...]
    zero_v = jnp.zeros((c, hd), BF16)
    for ci in range(GDN_PACK):
        r0 = ci * c
        slab = (r0 // hd) * hd
        wss = [_mm(jnp.concatenate([w_s[h, r0:r0 + c, :], qd_s[h, r0:r0 + c, :]], axis=0), state[h]) for h in hs]
        for h in hs:
            v_new = u_s[h, r0:r0 + c, :] - wss[h][0:c]
            v_new_b = v_new.astype(BF16)
            pieces = [zero_v] * (hd // c)
            pieces[(r0 - slab) // c] = v_new_b
            out = wss[h][c:2 * c] + jnp.dot(qk_s[h, r0:r0 + c, slab:slab + hd], jnp.concatenate(pieces, axis=0),
                                            preferred_element_type=F32)
            state[h] = (state[h] * jnp.exp(glast[r0:r0 + 1, h:h + 1])
                        + _mm_tn(kd_s[h, r0:r0 + c, :], v_new_b))
            o = out * lax.rsqrt(jnp.mean(out * out, axis=-1, keepdims=True) + RMS_EPS) * gain
            zz = z_ref[r0:r0 + c, h * hd:(h + 1) * hd].astype(F32)
            o_ref[r0:r0 + c, h * hd:(h + 1) * hd] = (o * _silu(zz)).astype(o_ref.dtype)


def _gdn(proj, ab, alog_row, dtb_row, gain_row, layer, batch, seq, heads, cols):
    width = heads * LANES
    tt = GDN_PACK * GDN_CHUNK
    ts = GDN_TILES * tt
    nt = seq // ts
    kern = functools.partial(_gdn_kernel, heads=heads)
    return pl.pallas_call(
        kern,
        grid=(batch, nt),
        in_specs=[
            pl.BlockSpec((ts, width), lambda b, t: (b * nt + t, cols[0])),
            pl.BlockSpec((ts, width), lambda b, t: (b * nt + t, cols[1])),
            pl.BlockSpec((ts, width), lambda b, t: (b * nt + t, cols[2])),
            pl.BlockSpec((ts, width), lambda b, t: (b * nt + t, cols[3])),
            pl.BlockSpec((ts, LANES), lambda b, t: (b * nt + t, 0)),
            pl.BlockSpec((None, 1, LANES), lambda b, t: (layer, 0, 0)),
            pl.BlockSpec((None, 1, LANES), lambda b, t: (layer, 0, 0)),
            pl.BlockSpec((None, 1, LANES), lambda b, t: (layer, 0, 0)),
        ],
        out_specs=pl.BlockSpec((ts, width), lambda b, t: (b * nt + t, 0)),
        out_shape=jax.ShapeDtypeStruct((batch * seq, width), BF16),
        scratch_shapes=[
            pltpu.VMEM((heads, LANES, LANES), F32),
            pltpu.VMEM((heads, tt, tt), BF16),
            pltpu.VMEM((heads, tt, 2 * LANES), BF16),
            pltpu.VMEM((heads, tt, LANES), F32),
            pltpu.VMEM((heads, tt, LANES), BF16),
            pltpu.VMEM((heads, tt, LANES), BF16),
            pltpu.VMEM((heads, tt, LANES), BF16),
        ],
        compiler_params=pltpu.CompilerParams(
            dimension_semantics=("arbitrary", "arbitrary"), vmem_limit_bytes=VMEM_LIMIT),
        name="gdn",
    )(proj, proj, proj, proj, ab, alog_row, dtb_row, gain_row)


def _lru_kernel(xc_ref, yb_ref, wa_ref, ba_ref, wx_ref, bx_ref, lam_ref, o_ref,
                abuf, bbuf, hcar, *, blocks, ts):
    bd = LANES
    width = blocks * bd

    @pl.when(pl.program_id(1) == 0)
    def _():
        hcar[...] = jnp.zeros_like(hcar)

    for n in range(blocks):
        lo = n * bd
        xcb = xc_ref[:, lo:lo + bd]
        xc = xcb.astype(F32)
        r = jax.nn.sigmoid(jnp.dot(xcb, wa_ref[n], preferred_element_type=F32) + ba_ref[0:1, lo:lo + bd])
        i = jax.nn.sigmoid(jnp.dot(xcb, wx_ref[n], preferred_element_type=F32) + bx_ref[0:1, lo:lo + bd])
        log_a = (-LRU_C) * r * _softplus(-lam_ref[0:1, lo:lo + bd])
        a = jnp.exp(log_a)
        one_minus_a2 = -jnp.tanh(log_a) * (1.0 + a * a)
        abuf[:, lo:lo + bd] = a
        bbuf[:, lo:lo + bd] = jnp.sqrt(one_minus_a2) * (i * xc)

    rowi = lax.broadcasted_iota(jnp.int32, (SUBLANES, width), 0)

    def group(gi, hprev):
        r0 = pl.multiple_of(gi * SUBLANES, SUBLANES)
        a = abuf[pl.ds(r0, SUBLANES), :]
        b = bbuf[pl.ds(r0, SUBLANES), :]
        s = 1
        while s < SUBLANES:
            keep = rowi >= s
            a_sh = jnp.where(keep, pltpu.roll(a, s, axis=0), 1.0)
            b_sh = jnp.where(keep, pltpu.roll(b, s, axis=0), 0.0)
            b = a * b_sh + b
            a = a * a_sh
            s *= 2
        bbuf[pl.ds(r0, SUBLANES), :] = a * hprev + b
        return a[SUBLANES - 1:SUBLANES, :] * hprev + b[SUBLANES - 1:SUBLANES, :]

    hcar[...] = lax.fori_loop(0, ts // SUBLANES, group, hcar[...], unroll=4)
    o_ref[...] = (bbuf[...] * _gelu_tanh(yb_ref[...].astype(F32))).astype(o_ref.dtype)


def _lru(proj, w_a, b_a, w_x, b_x, lam, layer, batch, seq, blocks, xb_col, yb_col, ts):
    width = blocks * LANES
    nt = seq // ts
    kern = functools.partial(_lru_kernel, blocks=blocks, ts=ts)
    row = lambda b, t: (layer, 0, 0)
    mat = lambda b, t: (layer, 0, 0, 0)
    return pl.pallas_call(
        kern,
        grid=(batch, nt),
        in_specs=[
            pl.BlockSpec((ts, width), lambda b, t: (b * nt + t, xb_col)),
            pl.BlockSpec((ts, width), lambda b, t: (b * nt + t, yb_col)),
            pl.BlockSpec((None, blocks, LANES, LANES), mat),
            pl.BlockSpec((None, 1, width), row),
            pl.BlockSpec((None, blocks, LANES, LANES), mat),
            pl.BlockSpec((None, 1, width), row),
            pl.BlockSpec((None, 1, width), row),
        ],
        out_specs=pl.BlockSpec((ts, width), lambda b, t: (b * nt + t, 0)),
        out_shape=jax.ShapeDtypeStruct((batch * seq, width), BF16),
        scratch_shapes=[
            pltpu.VMEM((ts, width), F32),
            pltpu.VMEM((ts, width), F32),
            pltpu.VMEM((1, width), F32),
        ],
        compiler_params=pltpu.CompilerParams(
            dimension_semantics=("arbitrary", "arbitrary"), vmem_limit_bytes=VMEM_LIMIT),
        name="lru",
    )(proj, proj, w_a, b_a, w_x, b_x, lam)


def _merge_kernel(x_ref, og_ref, ol_ref, gg0_ref, gg1_ref, gl_ref, wg_ref, wl_ref, wo_ref, c0_ref, c1_ref,
                  o_ref, d0_ref, d1_ref):
    _cast_step((c0_ref, c1_ref), (d0_ref, d1_ref))
    pg = jnp.dot(og_ref[...], wg_ref[...], preferred_element_type=F32)
    pb = jnp.dot(ol_ref[...], wl_ref[...], preferred_element_type=F32)
    gg = jnp.concatenate([gg0_ref[...], gg1_ref[...]], axis=1)
    merged = (jax.nn.sigmoid(gg.astype(F32)) * pg
              + jax.nn.sigmoid(gl_ref[...].astype(F32)) * pb)
    o_ref[...] = x_ref[...] + jnp.dot(merged.astype(BF16), wo_ref[...], preferred_element_type=F32)


def _merge(x2, o_gdn, o_lru, proj, w_bg, w_bl, w_out, casts, layer, gg_cols, gl_col, tm):
    m, d = x2.shape
    wg = o_gdn.shape[1]
    wl = o_lru.shape[1]
    const = lambda i: (0, 0)
    steps = m // tm
    cast = [_cast_specs(w, layer, steps, lambda i: i) for w in casts]
    return pl.pallas_call(
        _merge_kernel,
        grid=(steps,),
        in_specs=[
            pl.BlockSpec((tm, d), lambda i: (i, 0)),
            pl.BlockSpec((tm, wg), lambda i: (i, 0)),
            pl.BlockSpec((tm, wl), lambda i: (i, 0)),
            pl.BlockSpec((tm, d // 2), lambda i: (i, gg_cols[0])),
            pl.BlockSpec((tm, d // 2), lambda i: (i, gg_cols[1])),
            pl.BlockSpec((tm, d), lambda i: (i, gl_col)),
            pl.BlockSpec((wg, d), const, pipeline_mode=pl.Buffered(1)),
            pl.BlockSpec((wl, d), const, pipeline_mode=pl.Buffered(1)),
            pl.BlockSpec((d, d), const, pipeline_mode=pl.Buffered(1)),
        ] + [c[0] for c in cast],
        out_specs=[pl.BlockSpec((tm, d), lambda i: (i, 0))] + [c[1] for c in cast],
        out_shape=[jax.ShapeDtypeStruct((m, d), F32)] + [c[2] for c in cast],
        compiler_params=pltpu.CompilerParams(
            dimension_semantics=("arbitrary",), vmem_limit_bytes=VMEM_LIMIT),
        name="merge",
    )(x2, o_gdn, o_lru, proj, proj, proj, w_bg, w_bl, w_out, *casts)


def _mlp_kernel(x_ref, g_ref, wu_ref, wd_ref, fg_ref, o_ref, h_ref, *, final_norm):
    f = pl.program_id(1)

    @pl.when(f == 0)
    def _():
        x = x_ref[...]
        ms = jnp.mean(x * x, axis=-1, keepdims=True)
        h_ref[...] = (x * lax.rsqrt(ms + RMS_EPS) * g_ref[...]).astype(BF16)
        o_ref[...] = x

    u = jnp.dot(h_ref[...], wu_ref[...], preferred_element_type=F32)
    u = jnp.maximum(u, 0.0)
    o_ref[...] += jnp.dot((u * u).astype(BF16), wd_ref[...], preferred_element_type=F32)

    if final_norm:
        @pl.when(f == pl.num_programs(1) - 1)
        def _():
            y = o_ref[...]
            ms = jnp.mean(y * y, axis=-1, keepdims=True)
            o_ref[...] = y * lax.rsqrt(ms + RMS_EPS) * fg_ref[...]


def _mlp(x2, gain, w_up, w_down, final_gain, layer, final_norm, tm, tf):
    m, d = x2.shape
    ff = w_up.shape[1]
    kern = functools.partial(_mlp_kernel, final_norm=final_norm)
    return pl.pallas_call(
        kern,
        grid=(m // tm, ff // tf),
        in_specs=[
            pl.BlockSpec((tm, d), lambda i, f: (i, 0)),
            pl.BlockSpec((None, 1, d), lambda i, f: (layer, 0, 0)),
            pl.BlockSpec((d, tf), lambda i, f: (0, f)),
            pl.BlockSpec((tf, d), lambda i, f: (f, 0)),
            pl.BlockSpec((1, d), lambda i, f: (0, 0)),
        ],
        out_specs=pl.BlockSpec((tm, d), lambda i, f: (i, 0)),
        out_shape=jax.ShapeDtypeStruct((m, d), F32),
        scratch_shapes=[pltpu.VMEM((tm, d), BF16)],
        compiler_params=pltpu.CompilerParams(
            dimension_semantics=("arbitrary", "arbitrary"), vmem_limit_bytes=VMEM_LIMIT),
        name="mlp",
    )(x2, gain, w_up, w_down, final_gain)


def _tile(n, want):
    t = min(want, n)
    assert n % t == 0, (n, want)
    return t


def kernel(x, attn_norm, w_in, gdn_conv_w, gdn_a_log, gdn_dt_bias, gdn_norm, lru_conv_w, lru_conv_b,
           lru_w_a, lru_b_a, lru_w_x, lru_b_x, lru_lambda, w_branch_gdn, w_branch_lru, w_out,
           mlp_norm, w_up, w_down, final_norm):
    batch, seq, d = x.shape
    depth = w_in.shape[0]
    heads = gdn_a_log.shape[1]
    gw = heads * LANES
    assert gdn_norm.shape[1] == LANES and gdn_conv_w.shape[2] == 3 * gw
    blocks = lru_w_a.shape[1]
    lw = blocks * LANES
    assert lru_w_a.shape[2] == LANES
    assert 2 * heads <= LANES and seq % (GDN_TILES * GDN_PACK * GDN_CHUNK) == 0
    assert gw == lw
    m = batch * seq

    o_z = 3 * gw
    o_a = o_z + gw
    o_xb = o_a + 2 * heads
    o_yb = o_xb + lw
    o_gg = o_yb + lw
    o_gl = o_gg + d
    assert w_in.shape[2] == o_gl + d
    assert d == 2 * gw
    a_rows = (0, gw, 2 * gw, o_xb, o_gl)
    b_rows = (o_z, o_yb, o_gg, o_gg + gw, o_gl + gw)
    assert all(r % BF16_ROWS == 0 for r in a_rows + b_rows + (o_a,))
    gdn_cols = (0, 2, 4, 1)
    yb_col, xb_col = 3, 6
    gg_cols, gl_col = (5, 7), 4
    lru_step = 3

    tm_in = _tile(seq, 1024)
    ts_lru = _tile(seq, 1024)
    tm_merge = _tile(m, 256)
    tm_mlp = _tile(m, 1024)
    tf_mlp = _tile(w_up.shape[2], 1024)

    w_t = jnp.swapaxes(w_in, 1, 2).astype(BF16)
    zeros_w = jnp.zeros((depth, CONV_WIDTH, gw), F32)
    conv_w = jnp.concatenate([gdn_conv_w.astype(F32), lru_conv_w.astype(F32), zeros_w], axis=2)
    conv_w = jnp.pad(conv_w, ((0, 0), (0, SUBLANES - CONV_WIDTH), (0, 0)))
    conv_b = jnp.pad(lru_conv_b.astype(F32), ((0, 0), (3 * gw, gw)))[:, None, :]
    conv_b = jnp.broadcast_to(conv_b, (depth, SUBLANES, conv_b.shape[2]))
    attn_gain = jnp.broadcast_to(attn_norm.astype(F32)[:, None, :], (depth, SUBLANES, d))
    pad_h = ((0, 0), (0, LANES - heads))
    alog_rows = jnp.pad(gdn_a_log.astype(F32), pad_h)[:, None, :]
    dtb_rows = jnp.pad(gdn_dt_bias.astype(F32), pad_h)[:, None, :]
    w_a_b, w_x_b = lru_w_a.astype(BF16), lru_w_x.astype(BF16)
    row3 = lambda v: v[:, None, :]

    x2 = x.reshape(m, d)
    for l in range(depth):
        proj, ab, w_bg_b, w_bl_b, w_out_b = _in_proj(
            x2, attn_gain, w_t, conv_w, conv_b, (w_branch_gdn, w_branch_lru, w_out),
            l, tm_in, gw, seq, lru_step, a_rows, b_rows, o_a)
        o_gdn = _gdn(proj, ab, alog_rows, dtb_rows, row3(gdn_norm), l, batch, seq, heads, gdn_cols)
        o_lru = _lru(proj, w_a_b, row3(lru_b_a), w_x_b, row3(lru_b_x),
                     row3(lru_lambda), l, batch, seq, blocks, xb_col, yb_col, ts_lru)
        x2, w_up_b, w_down_b = _merge(x2, o_gdn, o_lru, proj, w_bg_b, w_bl_b, w_out_b, (w_up, w_down),
                                      l, gg_cols, gl_col, tm_merge)
        x2 = _mlp(x2, row3(mlp_norm), w_up_b, w_down_b, final_norm.reshape(1, d), l, l == depth - 1,
                  tm_mlp, tf_mlp)
    if depth == 0:
        raise ValueError("depth must be positive")
    return x2.reshape(batch, seq, d)
```

```python
import functools

import jax
import jax.numpy as jnp
from jax import lax
from jax.experimental import pallas as pl
from jax.experimental.pallas import tpu as pltpu

F32 = jnp.float32
BF16 = jnp.bfloat16

RMS_EPS = 1e-6
L2_EPS = 1e-6
LRU_C = 8.0
CONV_WIDTH = 4
GDN_CHUNK = 64
INV_BLOCK = 8
GDN_HEAD_GROUP = 8
GDN_PACK = 4
GDN_TILES = 2
CONV_COLS = 256
LANES = 128
SUBLANES = 8
BF16_ROWS = 16
VMEM_LIMIT = 60 * 1024 * 1024


def _mm(a, b):
    return jnp.dot(a.astype(BF16), b.astype(BF16), preferred_element_type=F32)


def _mm_tn(a, b):
    return lax.dot_general(a.astype(BF16), b.astype(BF16), (((0,), (0,)), ((), ())),
                           preferred_element_type=F32)


def _softplus(x):
    return jnp.maximum(x, 0.0) + jnp.log1p(jnp.exp(-jnp.abs(x)))


def _silu(x):
    h = 0.5 * x
    return h + h * jnp.tanh(h)


def _gelu_tanh(x):
    c = 0.7978845608028654
    return 0.5 * x * (1.0 + jnp.tanh(c * (x + 0.044715 * (x * x * x))))


def _cast_plan(w, steps):
    rows = w.shape[1]
    units = rows // BF16_ROWS
    assert rows % BF16_ROWS == 0
    nb = max(k for k in range(1, min(units, steps) + 1) if units % k == 0)
    return nb, rows // nb


def _cast_specs(w, layer, steps, step_of):
    nb, rb = _cast_plan(w, steps)
    cols = w.shape[2]
    blk = lambda *ids: jnp.minimum(step_of(*ids), nb - 1)
    return (pl.BlockSpec((None, rb, cols), lambda *ids: (layer, blk(*ids), 0)),
            pl.BlockSpec((rb, cols), lambda *ids: (blk(*ids), 0)),
            jax.ShapeDtypeStruct((w.shape[1], cols), BF16))


def _cast_step(srcs, dsts):
    for src, dst in zip(srcs, dsts):
        dst[...] = src[...].astype(dst.dtype)


def _dot_nt(a, b):
    return lax.dot_general(a, b, (((1,), (1,)), ((), ())), preferred_element_type=F32)


def _in_proj_kernel(x_ref, g_ref, wa_ref, wb_ref, wab_ref, cw_ref, cb_ref, c0_ref, c1_ref, c2_ref,
                    o_ref, oab_ref, d0_ref, d1_ref, d2_ref, h_ref, tail_ref, *,
                    tiles_per_seq, lru_step):
    i = pl.program_id(0)
    j = pl.program_id(1)
    tm, tn = o_ref.shape
    half = tn // 2

    def ride_casts():
        _cast_step((c0_ref, c1_ref, c2_ref), (d0_ref, d1_ref, d2_ref))

    @pl.when(j == 0)
    def _():
        x = x_ref[...]
        ms = jnp.mean(x * x, axis=-1, keepdims=True)
        h = (x * lax.rsqrt(ms + RMS_EPS) * g_ref[...]).astype(BF16)
        h_ref[...] = h
        oab_ref[...] = _dot_nt(h, wab_ref[...])

        @pl.when(i == 0)
        def _():
            tail_ref[...] = jnp.zeros_like(tail_ref)

    cc = min(half, CONV_COLS)
    row8 = lax.broadcasted_iota(jnp.int32, (SUBLANES, cc), 0)

    def shift_rows(val, prev8, s):
        rolled = pltpu.roll(val, s, axis=0)
        top = jnp.where(row8 < s, pltpu.roll(prev8, s, axis=0), rolled[0:SUBLANES, :])
        return jnp.concatenate([top, rolled[SUBLANES:, :]], axis=0)

    def conv_chunks(finish):
        ride_casts()
        first = i % tiles_per_seq == 0
        nb = half // cc
        for ci in range(nb):
            c0 = ci * cc
            col = pl.multiple_of(j * half, half) + c0
            prev = jnp.where(first, 0.0, tail_ref[:, pl.ds(col, cc)])
            res = _dot_nt(h_ref[...], wa_ref[c0:c0 + cc, :])
            tail_ref[:, pl.ds(col, cc)] = res[tm - SUBLANES:tm, :]
            b0 = half + c0
            o_ref[:, b0:b0 + cc] = _dot_nt(h_ref[...], wb_ref[c0:c0 + cc, :]).astype(o_ref.dtype)
            taps = [cw_ref[k:k + 1, c0:c0 + cc] for k in range(CONV_WIDTH)]
            s1 = shift_rows(res, prev, 1)
            u = res * taps[3] + s1 * taps[2]
            v = res * taps[1] + s1 * taps[0]
            prev_v = prev * taps[1] + pltpu.roll(prev, 1, axis=0) * taps[0]
            finish(u + shift_rows(v, prev_v, 2), c0)

    @pl.when(j < 3)
    def _():
        is_v = j == 2
        q_scale = jnp.where(j == 0, LANES ** -0.5, 1.0)

        def finish(acc, c0):
            y = _silu(acc)
            for g in range(cc // LANES):
                yg = y[:, g * LANES:(g + 1) * LANES]
                inv = lax.rsqrt(jnp.sum(yg * yg, axis=-1, keepdims=True) + L2_EPS)
                o_ref[:, c0 + g * LANES:c0 + (g + 1) * LANES] = (
                    yg * jnp.where(is_v, 1.0, inv * q_scale)).astype(o_ref.dtype)

        conv_chunks(finish)

    @pl.when(j == lru_step)
    def _():
        def finish(acc, c0):
            o_ref[:, c0:c0 + cc] = (acc + cb_ref[:, c0:c0 + cc]).astype(o_ref.dtype)

        conv_chunks(finish)

    @pl.when((j >= 3) & (j != lru_step))
    def _():
        ride_casts()
        o_ref[:, 0:half] = _dot_nt(h_ref[...], wa_ref[...]).astype(o_ref.dtype)
        o_ref[:, half:tn] = _dot_nt(h_ref[...], wb_ref[...]).astype(o_ref.dtype)


def _in_proj(x2, gain, w_t, conv_w, conv_b, casts, layer, tm, half, seq, lru_step, a_rows, b_rows, ab_row):
    m, d = x2.shape
    nj = len(a_rows)
    tn = 2 * half
    n = nj * tn
    steps = (m // tm) * nj
    cast = [_cast_specs(w, layer, steps, lambda i, j: i * nj + j) for w in casts]
    kern = functools.partial(_in_proj_kernel, tiles_per_seq=seq // tm, lru_step=lru_step)

    def rows_of(table):
        return lambda i, j: (layer, BF16_ROWS * sum(jnp.where(j == s, r // BF16_ROWS, 0)
                                                     for s, r in enumerate(table)), 0)

    w_tile = (None, pl.Element(half), pl.Element(d))
    return pl.pallas_call(
        kern,
        grid=(m // tm, nj),
        in_specs=[
            pl.BlockSpec((tm, d), lambda i, j: (i, 0)),
            pl.BlockSpec((None, 1, d), lambda i, j: (layer, 0, 0)),
            pl.BlockSpec(w_tile, rows_of(a_rows)),
            pl.BlockSpec(w_tile, rows_of(b_rows)),
            pl.BlockSpec((None, pl.Element(LANES), pl.Element(d)), lambda i, j: (layer, ab_row, 0)),
            pl.BlockSpec((None, CONV_WIDTH, half), lambda i, j: (layer, 0, j)),
            pl.BlockSpec((None, 1, half), lambda i, j: (layer, 0, j)),
        ] + [c[0] for c in cast],
        out_specs=[
            pl.BlockSpec((tm, tn), lambda i, j: (i, j)),
            pl.BlockSpec((tm, LANES), lambda i, j: (i, 0)),
        ] + [c[1] for c in cast],
        out_shape=[
            jax.ShapeDtypeStruct((m, n), BF16),
            jax.ShapeDtypeStruct((m, LANES), F32),
        ] + [c[2] for c in cast],
        scratch_shapes=[pltpu.VMEM((tm, d), BF16), pltpu.VMEM((SUBLANES, n // 2), F32)],
        compiler_params=pltpu.CompilerParams(
            dimension_semantics=("arbitrary", "arbitrary"), vmem_limit_bytes=VMEM_LIMIT),
        name="in_proj",
    )(x2, gain, w_t, w_t, w_t, conv_w, conv_b, *casts)


def _mixers_kernel(q_ref, k_ref, v_ref, z_ref, ab_ref, alog_ref, dtb_ref, gain_ref,
                   xc_ref, yb_ref, wa_ref, ba_ref, wx_ref, bx_ref, lam_ref, o_ref, ol_ref,
                   state, qk_s, rhs_s, u_s, w_s, qd_s, kd_s, abuf, bbuf, hcar, *, heads, blocks):
    ts = GDN_PACK * GDN_CHUNK

    @pl.when(pl.program_id(1) == 0)
    def _():
        state[...] = jnp.zeros_like(state)
        hcar[...] = jnp.zeros_like(hcar)

    _lru_gates(xc_ref, wa_ref, ba_ref, wx_ref, bx_ref, lam_ref, abuf, bbuf, blocks)
    for tile in range(q_ref.shape[0] // ts):
        rows = pl.ds(tile * ts, ts)
        _gdn_tile(q_ref.at[rows], k_ref.at[rows], v_ref.at[rows], z_ref.at[rows], ab_ref.at[rows],
                  alog_ref, dtb_ref, gain_ref, o_ref.at[rows], state, qk_s, rhs_s, u_s, w_s, qd_s, kd_s, heads)
    _lru_scan(yb_ref, ol_ref, abuf, bbuf, hcar)


def _gdn_tile(q_ref, k_ref, v_ref, z_ref, ab_ref, alog_ref, dtb_ref, gain_ref, o_ref,
              state, qk_s, rhs_s, u_s, w_s, qd_s, kd_s, heads):
    hd = LANES
    width = heads * hd
    c = GDN_CHUNK
    ts = GDN_PACK * c
    hs = range(heads)

    ab = ab_ref[...]
    g_all = -jnp.exp(alog_ref[...]) * _softplus(ab + dtb_ref[...])
    beta_all = jax.nn.sigmoid(ab)

    r2 = lax.broadcasted_iota(jnp.int32, (ts, ts), 0)
    c2 = lax.broadcasted_iota(jnp.int32, (ts, ts), 1)
    same_chunk = (r2 & -c) == (c2 & -c)
    causal_bd = same_chunk & (r2 >= c2)
    strict_bd = same_chunk & (r2 > c2)
    rp = lax.broadcasted_iota(jnp.int32, (c, ts), 0)
    cp = lax.broadcasted_iota(jnp.int32, (c, ts), 1) & (c - 1)
    eye_p = jnp.where(rp == cp, 1.0, 0.0)
    in_diag_block = (rp & -INV_BLOCK) == (cp & -INV_BLOCK)

    lane = lax.broadcasted_iota(jnp.int32, (c, hd), 1)
    lane_masks = [jnp.where((lane & -c) == p * c, 1.0, 0.0).astype(BF16) for p in range(hd // c)]
    zero_blk = jnp.zeros((c, hd), BF16)

    def blockdiag(y):
        yb = y.astype(BF16)
        rows = []
        for i in range(GDN_PACK):
            slab, off = divmod(i * c, hd)
            blocks = [zero_blk] * (ts // hd)
            blocks[slab] = yb[:, slab * hd:(slab + 1) * hd] * lane_masks[off // c]
            rows.append(jnp.concatenate(blocks, axis=1))
        return jnp.concatenate(rows, axis=0)

    def bmm(x, bd):
        return jnp.dot(x.astype(BF16), bd, preferred_element_type=F32)

    g_hi = g_all.astype(BF16)
    r_mid = g_all - g_hi.astype(F32)
    g_mid = r_mid.astype(BF16)
    g_lo = (r_mid - g_mid.astype(F32)).astype(BF16)
    sums = jnp.dot(jnp.where(causal_bd, 1.0, 0.0).astype(BF16), jnp.concatenate([g_hi, g_mid, g_lo], axis=1),
                   preferred_element_type=F32)
    gcum = sums[:, 0:LANES] + sums[:, LANES:2 * LANES] + sums[:, 2 * LANES:3 * LANES]
    glast = jnp.concatenate([jnp.broadcast_to(gcum[(ci + 1) * c - 1:(ci + 1) * c, :], (c, LANES))
                             for ci in range(GDN_PACK)], axis=0)
    gcum_t = gcum.T

    for g0 in range(0, heads, GDN_HEAD_GROUP):
        hg = range(g0, min(g0 + GDN_HEAD_GROUP, heads))
        prods = []
        for h in hg:
            qb = q_ref[:, h * hd:(h + 1) * hd]
            kb16 = k_ref[:, h * hd:(h + 1) * hd]
            q = qb.astype(F32)
            k = kb16.astype(F32)
            v = v_ref[:, h * hd:(h + 1) * hd].astype(F32)
            gcol = gcum[:, h:h + 1]
            egc = jnp.exp(gcol)
            beta = beta_all[:, heads + h:heads + h + 1]
            kb = k * beta
            prods.append(lax.dot_general(jnp.concatenate([qb, kb.astype(BF16)], axis=0), kb16,
                                         (((1,), (1,)), ((), ())), preferred_element_type=F32))
            rhs_s[h] = jnp.concatenate([v * beta, kb * egc], axis=1).astype(BF16)
            qd_s[h] = (q * egc).astype(BF16)
            kd_s[h] = (k * jnp.exp(glast[:, h:h + 1] - gcol)).astype(BF16)
        lps = []
        for n, h in enumerate(hg):
            gcol = gcum[:, h:h + 1]
            grow = gcum_t[h:h + 1, :]
            dmat = jnp.where(causal_bd, jnp.exp(jnp.where(causal_bd, gcol - grow, 0.0)), 0.0)
            qk_s[h] = (prods[n][0:ts] * dmat).astype(BF16)
            lbd = jnp.where(strict_bd, prods[n][ts:2 * ts] * dmat, 0.0)
            lp = lbd[0:c]
            for i in range(1, GDN_PACK):
                lp = lp + lbd[i * c:(i + 1) * c]
            lps.append(lp)

        ld = [jnp.where(in_diag_block, lp, 0.0) for lp in lps]
        tinv = [eye_p - d for d in ld]
        pw = [bmm(d, blockdiag(d)) for d in ld]
        span = 2
        while span < INV_BLOCK:
            last = 2 * span >= INV_BLOCK
            res = [bmm(d if last else jnp.concatenate([p, d], axis=0), blockdiag(p)) for p, d in zip(pw, tinv)]
            if last:
                tinv = [d + r for d, r in zip(tinv, res)]
            else:
                pw = [r[0:c] for r in res]
                tinv = [d + r[c:2 * c] for d, r in zip(tinv, res)]
            span *= 2
        b = INV_BLOCK
        while b < c:
            off = ((rp & -2 * b) == (cp & -2 * b)) & ((rp & -b) != (cp & -b))
            te = [bmm(x, blockdiag(jnp.where(off, lp, 0.0))) for x, lp in zip(tinv, lps)]
            tinv = [x - bmm(y, blockdiag(x)) for x, y in zip(tinv, te)]
            b *= 2

        for n, h in enumerate(hg):
            uw = jnp.dot(blockdiag(tinv[n]), rhs_s[h], preferred_element_type=F32)
            u_s[h] = uw[:, 0:hd]
            w_s[h] = uw[:, hd:2 * hd].astype(BF16)

    gain = gain_ref[...]
    zero_v = jnp.zeros((c, hd), BF16)
    for ci in range(GDN_PACK):
        r0 = ci * c
        slab = (r0 // hd) * hd
        wss = [_mm(jnp.concatenate([w_s[h, r0:r0 + c, :], qd_s[h, r0:r0 + c, :]], axis=0), state[h]) for h in hs]
        for h in hs:
            v_new = u_s[h, r0:r0 + c, :] - wss[h][0:c]
            v_new_b = v_new.astype(BF16)
            pieces = [zero_v] * (hd // c)
            pieces[(r0 - slab) // c] = v_new_b
            out = wss[h][c:2 * c] + jnp.dot(qk_s[h, r0:r0 + c, slab:slab + hd], jnp.concatenate(pieces, axis=0),
                                            preferred_element_type=F32)
            state[h] = (state[h] * jnp.exp(glast[r0:r0 + 1, h:h + 1])
                        + _mm_tn(kd_s[h, r0:r0 + c, :], v_new_b))
            o = out * lax.rsqrt(jnp.mean(out * out, axis=-1, keepdims=True) + RMS_EPS) * gain
            zz = z_ref[r0:r0 + c, h * hd:(h + 1) * hd].astype(F32)
            o_ref[r0:r0 + c, h * hd:(h + 1) * hd] = (o * _silu(zz)).astype(o_ref.dtype)


def _mixers(proj, ab, alog_row, dtb_row, gain_row, w_a, b_a, w_x, b_x, lam, layer, batch, seq, heads, blocks,
            cols, xb_col, yb_col):
    width = heads * LANES
    assert blocks * LANES == width
    tt = GDN_PACK * GDN_CHUNK
    ts = GDN_TILES * tt
    nt = seq // ts
    kern = functools.partial(_mixers_kernel, heads=heads, blocks=blocks)
    row = lambda b, t: (layer, 0, 0)
    mat = lambda b, t: (layer, 0, 0, 0)
    tile = lambda b, t: (b * nt + t, 0)
    return pl.pallas_call(
        kern,
        grid=(batch, nt),
        in_specs=[
            pl.BlockSpec((ts, width), lambda b, t: (b * nt + t, cols[0])),
            pl.BlockSpec((ts, width), lambda b, t: (b * nt + t, cols[1])),
            pl.BlockSpec((ts, width), lambda b, t: (b * nt + t, cols[2])),
            pl.BlockSpec((ts, width), lambda b, t: (b * nt + t, cols[3])),
            pl.BlockSpec((ts, LANES), lambda b, t: (b * nt + t, 0)),
            pl.BlockSpec((None, 1, LANES), row),
            pl.BlockSpec((None, 1, LANES), row),
            pl.BlockSpec((None, 1, LANES), row),
            pl.BlockSpec((ts, width), lambda b, t: (b * nt + t, xb_col)),
            pl.BlockSpec((ts, width), lambda b, t: (b * nt + t, yb_col)),
            pl.BlockSpec((None, blocks, LANES, LANES), mat),
            pl.BlockSpec((None, 1, width), row),
            pl.BlockSpec((None, blocks, LANES, LANES), mat),
            pl.BlockSpec((None, 1, width), row),
            pl.BlockSpec((None, 1, width), row),
        ],
        out_specs=[pl.BlockSpec((ts, width), tile), pl.BlockSpec((ts, width), tile)],
        out_shape=[jax.ShapeDtypeStruct((batch * seq, width), BF16),
                   jax.ShapeDtypeStruct((batch * seq, width), BF16)],
        scratch_shapes=[
            pltpu.VMEM((heads, LANES, LANES), F32),
            pltpu.VMEM((heads, tt, tt), BF16),
            pltpu.VMEM((heads, tt, 2 * LANES), BF16),
            pltpu.VMEM((heads, tt, LANES), F32),
            pltpu.VMEM((heads, tt, LANES), BF16),
            pltpu.VMEM((heads, tt, LANES), BF16),
            pltpu.VMEM((heads, tt, LANES), BF16),
            pltpu.VMEM((ts, width), F32),
            pltpu.VMEM((ts, width), F32),
            pltpu.VMEM((1, width), F32),
        ],
        compiler_params=pltpu.CompilerParams(
            dimension_semantics=("arbitrary", "arbitrary"), vmem_limit_bytes=VMEM_LIMIT),
        name="mixers",
    )(proj, proj, proj, proj, ab, alog_row, dtb_row, gain_row, proj, proj, w_a, b_a, w_x, b_x, lam)


def _lru_gates(xc_ref, wa_ref, ba_ref, wx_ref, bx_ref, lam_ref, abuf, bbuf, blocks):
    bd = LANES
    for n in range(blocks):
        lo = n * bd
        xcb = xc_ref[:, lo:lo + bd]
        xc = xcb.astype(F32)
        r = jax.nn.sigmoid(jnp.dot(xcb, wa_ref[n], preferred_element_type=F32) + ba_ref[0:1, lo:lo + bd])
        i = jax.nn.sigmoid(jnp.dot(xcb, wx_ref[n], preferred_element_type=F32) + bx_ref[0:1, lo:lo + bd])
        log_a = (-LRU_C) * r * _softplus(-lam_ref[0:1, lo:lo + bd])
        a = jnp.exp(log_a)
        one_minus_a2 = -jnp.tanh(log_a) * (1.0 + a * a)
        abuf[:, lo:lo + bd] = a
        bbuf[:, lo:lo + bd] = jnp.sqrt(one_minus_a2) * (i * xc)


def _lru_scan(yb_ref, o_ref, abuf, bbuf, hcar):
    ts, width = abuf.shape
    rowi = lax.broadcasted_iota(jnp.int32, (SUBLANES, width), 0)

    def group(gi, hprev):
        r0 = pl.multiple_of(gi * SUBLANES, SUBLANES)
        a = abuf[pl.ds(r0, SUBLANES), :]
        b = bbuf[pl.ds(r0, SUBLANES), :]
        s = 1
        while s < SUBLANES:
            keep = rowi >= s
            a_sh = jnp.where(keep, pltpu.roll(a, s, axis=0), 1.0)
            b_sh = jnp.where(keep, pltpu.roll(b, s, axis=0), 0.0)
            b = a * b_sh + b
            a = a * a_sh
            s *= 2
        bbuf[pl.ds(r0, SUBLANES), :] = a * hprev + b
        return a[SUBLANES - 1:SUBLANES, :] * hprev + b[SUBLANES - 1:SUBLANES, :]

    hcar[...] = lax.fori_loop(0, ts // SUBLANES, group, hcar[...], unroll=4)
    o_ref[...] = (bbuf[...] * _gelu_tanh(yb_ref[...].astype(F32))).astype(o_ref.dtype)


def _merge_kernel(x_ref, og_ref, ol_ref, gg0_ref, gg1_ref, gl_ref, wg_ref, wl_ref, wo_ref, c0_ref, c1_ref,
                  o_ref, d0_ref, d1_ref):
    _cast_step((c0_ref, c1_ref), (d0_ref, d1_ref))
    pg = jnp.dot(og_ref[...], wg_ref[...], preferred_element_type=F32)
    pb = jnp.dot(ol_ref[...], wl_ref[...], preferred_element_type=F32)
    gg = jnp.concatenate([gg0_ref[...], gg1_ref[...]], axis=1)
    merged = (jax.nn.sigmoid(gg.astype(F32)) * pg
              + jax.nn.sigmoid(gl_ref[...].astype(F32)) * pb)
    o_ref[...] = x_ref[...] + jnp.dot(merged.astype(BF16), wo_ref[...], preferred_element_type=F32)


def _merge(x2, o_gdn, o_lru, proj, w_bg, w_bl, w_out, casts, layer, gg_cols, gl_col, tm):
    m, d = x2.shape
    wg = o_gdn.shape[1]
    wl = o_lru.shape[1]
    const = lambda i: (0, 0)
    steps = m // tm
    cast = [_cast_specs(w, layer, steps, lambda i: i) for w in casts]
    return pl.pallas_call(
        _merge_kernel,
        grid=(steps,),
        in_specs=[
            pl.BlockSpec((tm, d), lambda i: (i, 0)),
            pl.BlockSpec((tm, wg), lambda i: (i, 0)),
            pl.BlockSpec((tm, wl), lambda i: (i, 0)),
            pl.BlockSpec((tm, d // 2), lambda i: (i, gg_cols[0])),
            pl.BlockSpec((tm, d // 2), lambda i: (i, gg_cols[1])),
            pl.BlockSpec((tm, d), lambda i: (i, gl_col)),
            pl.BlockSpec((wg, d), const, pipeline_mode=pl.Buffered(1)),
            pl.BlockSpec((wl, d), const, pipeline_mode=pl.Buffered(1)),
            pl.BlockSpec((d, d), const, pipeline_mode=pl.Buffered(1)),
        ] + [c[0] for c in cast],
        out_specs=[pl.BlockSpec((tm, d), lambda i: (i, 0))] + [c[1] for c in cast],
        out_shape=[jax.ShapeDtypeStruct((m, d), F32)] + [c[2] for c in cast],
        compiler_params=pltpu.CompilerParams(
            dimension_semantics=("arbitrary",), vmem_limit_bytes=VMEM_LIMIT),
        name="merge",
    )(x2, o_gdn, o_lru, proj, proj, proj, w_bg, w_bl, w_out, *casts)


def _mlp_kernel(x_ref, g_ref, wu_ref, wd_ref, fg_ref, o_ref, h_ref, *, final_norm):
    f = pl.program_id(1)

    @pl.when(f == 0)
    def _():
        x = x_ref[...]
        ms = jnp.mean(x * x, axis=-1, keepdims=True)
        h_ref[...] = (x * lax.rsqrt(ms + RMS_EPS) * g_ref[...]).astype(BF16)
        o_ref[...] = x

    u = jnp.dot(h_ref[...], wu_ref[...], preferred_element_type=F32)
    u = jnp.maximum(u, 0.0)
    o_ref[...] += jnp.dot((u * u).astype(BF16), wd_ref[...], preferred_element_type=F32)

    if final_norm:
        @pl.when(f == pl.num_programs(1) - 1)
        def _():
            y = o_ref[...]
            ms = jnp.mean(y * y, axis=-1, keepdims=True)
            o_ref[...] = y * lax.rsqrt(ms + RMS_EPS) * fg_ref[...]


def _mlp(x2, gain, w_up, w_down, final_gain, layer, final_norm, tm, tf):
    m, d = x2.shape
    ff = w_up.shape[1]
    kern = functools.partial(_mlp_kernel, final_norm=final_norm)
    return pl.pallas_call(
        kern,
        grid=(m // tm, ff // tf),
        in_specs=[
            pl.BlockSpec((tm, d), lambda i, f: (i, 0)),
            pl.BlockSpec((None, 1, d), lambda i, f: (layer, 0, 0)),
            pl.BlockSpec((d, tf), lambda i, f: (0, f)),
            pl.BlockSpec((tf, d), lambda i, f: (f, 0)),
            pl.BlockSpec((1, d), lambda i, f: (0, 0)),
        ],
        out_specs=pl.BlockSpec((tm, d), lambda i, f: (i, 0)),
        out_shape=jax.ShapeDtypeStruct((m, d), F32),
        scratch_shapes=[pltpu.VMEM((tm, d), BF16)],
        compiler_params=pltpu.CompilerParams(
            dimension_semantics=("arbitrary", "arbitrary"), vmem_limit_bytes=VMEM_LIMIT),
        name="mlp",
    )(x2, gain, w_up, w_down, final_gain)


def _tile(n, want):
    t = min(want, n)
    assert n % t == 0, (n, want)
    return t


def kernel(x, attn_norm, w_in, gdn_conv_w, gdn_a_log, gdn_dt_bias, gdn_norm, lru_conv_w, lru_conv_b,
           lru_w_a, lru_b_a, lru_w_x, lru_b_x, lru_lambda, w_branch_gdn, w_branch_lru, w_out,
           mlp_norm, w_up, w_down, final_norm):
    batch, seq, d = x.shape
    depth = w_in.shape[0]
    heads = gdn_a_log.shape[1]
    gw = heads * LANES
    assert gdn_norm.shape[1] == LANES and gdn_conv_w.shape[2] == 3 * gw
    blocks = lru_w_a.shape[1]
    lw = blocks * LANES
    assert lru_w_a.shape[2] == LANES
    assert 2 * heads <= LANES and seq % (GDN_TILES * GDN_PACK * GDN_CHUNK) == 0
    assert gw == lw
    m = batch * seq

    o_z = 3 * gw
    o_a = o_z + gw
    o_xb = o_a + 2 * heads
    o_yb = o_xb + lw
    o_gg = o_yb + lw
    o_gl = o_gg + d
    assert w_in.shape[2] == o_gl + d
    assert d == 2 * gw
    a_rows = (0, gw, 2 * gw, o_xb, o_gl)
    b_rows = (o_z, o_yb, o_gg, o_gg + gw, o_gl + gw)
    assert all(r % BF16_ROWS == 0 for r in a_rows + b_rows + (o_a,))
    gdn_cols = (0, 2, 4, 1)
    yb_col, xb_col = 3, 6
    gg_cols, gl_col = (5, 7), 4
    lru_step = 3

    tm_in = _tile(seq, 1024)
    tm_merge = _tile(m, 256)
    tm_mlp = _tile(m, 1024)
    tf_mlp = _tile(w_up.shape[2], 1024)

    w_t = jnp.swapaxes(w_in, 1, 2).astype(BF16)
    zeros_w = jnp.zeros((depth, CONV_WIDTH, gw), F32)
    conv_w = jnp.concatenate([gdn_conv_w.astype(F32), lru_conv_w.astype(F32), zeros_w], axis=2)
    conv_b = jnp.pad(lru_conv_b.astype(F32), ((0, 0), (3 * gw, gw)))[:, None, :]
    pad_h = ((0, 0), (0, LANES - heads))
    alog_rows = jnp.pad(gdn_a_log.astype(F32), pad_h)[:, None, :]
    dtb_rows = jnp.pad(gdn_dt_bias.astype(F32), pad_h)[:, None, :]
    w_a_b, w_x_b = lru_w_a.astype(BF16), lru_w_x.astype(BF16)
    row3 = lambda v: v[:, None, :]

    x2 = x.reshape(m, d)
    for l in range(depth):
        proj, ab, w_bg_b, w_bl_b, w_out_b = _in_proj(
            x2, row3(attn_norm), w_t, conv_w, conv_b, (w_branch_gdn, w_branch_lru, w_out),
            l, tm_in, gw, seq, lru_step, a_rows, b_rows, o_a)
        o_gdn, o_lru = _mixers(proj, ab, alog_rows, dtb_rows, row3(gdn_norm), w_a_b, row3(lru_b_a), w_x_b,
                               row3(lru_b_x), row3(lru_lambda), l, batch, seq, heads, blocks, gdn_cols,
                               xb_col, yb_col)
        x2, w_up_b, w_down_b = _merge(x2, o_gdn, o_lru, proj, w_bg_b, w_bl_b, w_out_b, (w_up, w_down),
                                      l, gg_cols, gl_col, tm_merge)
        x2 = _mlp(x2, row3(mlp_norm), w_up_b, w_down_b, final_norm.reshape(1, d), l, l == depth - 1,
                  tm_mlp, tf_mlp)
    if depth == 0:
        raise ValueError("depth must be positive")
    return x2.reshape(batch, seq, d)
```
